```python
import jax, jax.numpy as jnp
from jax import lax
import numpy as np

D_MODEL = 2048
BATCH = 8
SEQ = 2048
DEPTH = 1

D_MIX = D_MODEL
D_CONV = D_MIX // 2
CONV_WIDTH = 31
N_HEADS = 8
QK_NOPE_DIM = 128
QK_ROPE_DIM = 64
V_HEAD_DIM = 128
Q_LORA_RANK = 768
KV_LORA_RANK = 512
D_ATTN = N_HEADS * V_HEAD_DIM
D_IN = 2 * D_CONV + Q_LORA_RANK + KV_LORA_RANK + QK_ROPE_DIM
D_FF = 5632
FFN_RES_WEIGHT = 0.5
N_SUBLAYERS = 3
ROPE_BASE = 10000.0
Q_BLOCK = 128
EPS = 1e-6
POS_OFFSET_MAX = 4096
ADA_SCALE = 0.5

kernel_name = 'hymba_conformer_mla_macaron_adaln'


def rms_norm(x, g):
    xf = x.astype(jnp.float32)
    y = xf * lax.rsqrt(jnp.mean(xf * xf, axis=-1, keepdims=True) + EPS)
    return (y * g.astype(jnp.float32)).astype(x.dtype)


def layer_norm(x, g, b):
    xf = x.astype(jnp.float32)
    mu = jnp.mean(xf, axis=-1, keepdims=True)
    xc = xf - mu
    var = jnp.mean(xc * xc, axis=-1, keepdims=True)
    y = xc * lax.rsqrt(var + EPS) * g.astype(jnp.float32) + b.astype(jnp.float32)
    return y.astype(x.dtype)


def modulate(u, shift, scale):
    return u * (1.0 + scale[:, None, :]) + shift[:, None, :]


def swiglu(u, w_gate, w_up, w_down):
    return (jax.nn.silu(u @ w_gate) * (u @ w_up)) @ w_down


def apply_rope(x, cos, sin):
    half = x.shape[-1] // 2
    x1, x2 = x[..., :half], x[..., half:]
    cos = cos.astype(x.dtype)
    sin = sin.astype(x.dtype)
    return jnp.concatenate([x1 * cos - x2 * sin, x2 * cos + x1 * sin], axis=-1)


def conformer_conv(a, w_dw, b_dw, ln_g, ln_b):
    val, gate = jnp.split(a, 2, axis=-1)
    h = val * jax.nn.sigmoid(gate)
    h = lax.conv_general_dilated(
        h, w_dw[:, None, :].astype(h.dtype), window_strides=(1,),
        padding=[(CONV_WIDTH - 1, 0)],
        dimension_numbers=('NWC', 'WIO', 'NWC'),
        feature_group_count=D_CONV) + b_dw
    return jax.nn.silu(layer_norm(h, ln_g, ln_b))


def mla_causal_attention(q_nope, q_rope, k_nope, k_rope, v):
    S = q_nope.shape[1]
    scale = (QK_NOPE_DIM + QK_ROPE_DIM) ** -0.5
    outs = []
    for i in range(S // Q_BLOCK):
        q0 = i * Q_BLOCK
        kend = q0 + Q_BLOCK
        s = (jnp.einsum('bqhd,bkhd->bhqk', q_nope[:, q0:kend], k_nope[:, :kend])
             + jnp.einsum('bqhr,bkr->bhqk', q_rope[:, q0:kend], k_rope[:, :kend]))
        s = s.astype(jnp.float32) * scale
        qpos = q0 + jnp.arange(Q_BLOCK)
        mask = qpos[:, None] >= jnp.arange(kend)[None, :]
        s = jnp.where(mask[None, None], s, -jnp.inf)
        p = jax.nn.softmax(s, axis=-1).astype(v.dtype)
        outs.append(jnp.einsum('bhqk,bkhd->bqhd', p, v[:, :kend]))
    return jnp.concatenate(outs, axis=1)


def token_mix(u, cos, sin, w_in, w_dw, b_dw, ln_conv_g, ln_conv_b, g_q_lat, w_uq,
              g_kv_lat, w_uk, w_uv, g_conv_out, g_attn_out, w_out):
    B, S, _ = u.shape
    z = u @ w_in
    i1 = 2 * D_CONV
    i2 = i1 + Q_LORA_RANK
    i3 = i2 + KV_LORA_RANK
    conv_in = z[..., :i1]
    q_lat = z[..., i1:i2]
    kv_lat = z[..., i2:i3]
    k_rope_raw = z[..., i3:]
    conv_out = conformer_conv(conv_in, w_dw, b_dw, ln_conv_g, ln_conv_b)
    q = (rms_norm(q_lat, g_q_lat) @ w_uq).reshape(B, S, N_HEADS, QK_NOPE_DIM + QK_ROPE_DIM)
    q_nope = q[..., :QK_NOPE_DIM]
    q_rope = apply_rope(q[..., QK_NOPE_DIM:], cos, sin)
    c_kv = rms_norm(kv_lat, g_kv_lat)
    k_nope = (c_kv @ w_uk).reshape(B, S, N_HEADS, QK_NOPE_DIM)
    v = (c_kv @ w_uv).reshape(B, S, N_HEADS, V_HEAD_DIM)
    k_rope = apply_rope(k_rope_raw[:, :, None, :], cos, sin)[:, :, 0, :]
    attn = mla_causal_attention(q_nope, q_rope, k_nope, k_rope, v).reshape(B, S, D_ATTN)
    merged = jnp.concatenate([rms_norm(conv_out, g_conv_out), rms_norm(attn, g_attn_out)], axis=-1)
    return merged @ w_out


def setup_inputs(seed: int = 0) -> dict:
    key = jax.random.key(seed)
    ks = iter(jax.random.split(key, 40))
    L, D = DEPTH, D_MODEL
    f32 = jnp.float32

    def dense(shape, fan_in, mult=1.0):
        return jax.random.normal(next(ks), shape, f32) * (mult * fan_in ** -0.5)

    def gain(n):
        return 1.0 + 0.05 * jax.random.normal(next(ks), (L, n), f32)

    def bias(n):
        return 0.02 * jax.random.normal(next(ks), (L, n), f32)

    x = jax.random.normal(next(ks), (BATCH, SEQ, D), f32)
    c = jax.random.normal(next(ks), (BATCH, D), f32)
    offset = jax.random.randint(next(ks), (BATCH, 1), 0, POS_OFFSET_MAX, dtype=jnp.int32)
    positions = offset + jnp.arange(SEQ, dtype=jnp.int32)[None, :]
    return {
        'x': x,
        'c': c,
        'positions': positions,
        'w_ada': dense((L, D, N_SUBLAYERS * 3 * D), D, ADA_SCALE),
        'b_ada': bias(N_SUBLAYERS * 3 * D),
        'g_pre_ffn1': gain(D),
        'w1_gate': dense((L, D, D_FF), D),
        'w1_up': dense((L, D, D_FF), D),
        'w1_down': dense((L, D_FF, D), D_FF),
        'g_post_ffn1': gain(D),
        'g_pre_mix': gain(D),
        'w_in': dense((L, D, D_IN), D),
        'w_dw': dense((L, CONV_WIDTH, D_CONV), CONV_WIDTH),
        'b_dw': bias(D_CONV),
        'ln_conv_g': gain(D_CONV),
        'ln_conv_b': bias(D_CONV),
        'g_q_lat': gain(Q_LORA_RANK),
        'w_uq': dense((L, Q_LORA_RANK, N_HEADS * (QK_NOPE_DIM + QK_ROPE_DIM)), Q_LORA_RANK),
        'g_kv_lat': gain(KV_LORA_RANK),
        'w_uk': dense((L, KV_LORA_RANK, N_HEADS * QK_NOPE_DIM), KV_LORA_RANK),
        'w_uv': dense((L, KV_LORA_RANK, N_HEADS * V_HEAD_DIM), KV_LORA_RANK),
        'g_conv_out': gain(D_CONV),
        'g_attn_out': gain(D_ATTN),
        'w_out': dense((L, D_MIX, D), D_MIX),
        'g_post_mix': gain(D),
        'g_pre_ffn2': gain(D),
        'w2_gate': dense((L, D, D_FF), D),
        'w2_up': dense((L, D, D_FF), D),
        'w2_down': dense((L, D_FF, D), D_FF),
        'g_post_ffn2': gain(D),
    }


def reference(x, c, positions, w_ada, b_ada, g_pre_ffn1, w1_gate, w1_up, w1_down, g_post_ffn1,
              g_pre_mix, w_in, w_dw, b_dw, ln_conv_g, ln_conv_b, g_q_lat, w_uq, g_kv_lat, w_uk, w_uv,
              g_conv_out, g_attn_out, w_out, g_post_mix, g_pre_ffn2, w2_gate, w2_up, w2_down, g_post_ffn2):
    B, S, D = x.shape
    half = QK_ROPE_DIM // 2
    inv_freq = ROPE_BASE ** (-jnp.arange(half, dtype=jnp.float32) / half)
    ang = positions.astype(jnp.float32)[:, :, None, None] * inv_freq
    cos, sin = jnp.cos(ang), jnp.sin(ang)
    sc = jax.nn.silu(c)
    for l in range(DEPTH):
        mod = (sc @ w_ada[l] + b_ada[l]).reshape(B, N_SUBLAYERS, 3, D)
        u = modulate(rms_norm(x, g_pre_ffn1[l]), mod[:, 0, 0], mod[:, 0, 1])
        y = swiglu(u, w1_gate[l], w1_up[l], w1_down[l])
        x = x + FFN_RES_WEIGHT * mod[:, 0, 2][:, None, :] * rms_norm(y, g_post_ffn1[l])
        u = modulate(rms_norm(x, g_pre_mix[l]), mod[:, 1, 0], mod[:, 1, 1])
        y = token_mix(u, cos, sin, w_in[l], w_dw[l], b_dw[l], ln_conv_g[l], ln_conv_b[l],
                      g_q_lat[l], w_uq[l], g_kv_lat[l], w_uk[l], w_uv[l],
                      g_conv_out[l], g_attn_out[l], w_out[l])
        x = x + mod[:, 1, 2][:, None, :] * rms_norm(y, g_post_mix[l])
        u = modulate(rms_norm(x, g_pre_ffn2[l]), mod[:, 2, 0], mod[:, 2, 1])
        y = swiglu(u, w2_gate[l], w2_up[l], w2_down[l])
        x = x + FFN_RES_WEIGHT * mod[:, 2, 2][:, None, :] * rms_norm(y, g_post_ffn2[l])
    return x
```

```python
import functools

import jax
import jax.numpy as jnp
import numpy as np
from jax import lax
from jax.experimental import pallas as pl
from jax.experimental.pallas import tpu as pltpu

D_MODEL = 2048
D_CONV = 1024
CONV_WIDTH = 31
N_HEADS = 8
QK_NOPE_DIM = 128
QK_ROPE_DIM = 64
V_HEAD_DIM = 128
Q_LORA_RANK = 768
KV_LORA_RANK = 512
D_ATTN = N_HEADS * V_HEAD_DIM
D_FF = 5632
FFN_RES_WEIGHT = 0.5
N_SUBLAYERS = 3
ROPE_BASE = 10000.0
EPS = 1e-6

LANES = 128
SUBLANES = 8
VMEM_LIMIT = 56 * 1024 * 1024

TM = 512
TF = 512
TN_ADA = 1024
TQ = 512
HALO = 32
CONV_CB = 256
CONV_R = 64

F32 = jnp.float32
BF16 = jnp.bfloat16


def _rms(x, g):
    return x * lax.rsqrt(jnp.mean(x * x, axis=-1, keepdims=True) + EPS) * g


def _prenorm_modulate(x, g, mod_ref, sub):
    shift = mod_ref[0, 3 * sub:3 * sub + 1, :]
    scale = mod_ref[0, 3 * sub + 1:3 * sub + 2, :]
    return _rms(x, g) * (1.0 + scale) + shift


def _dot(a, b):
    return jnp.dot(a, b, preferred_element_type=F32)


def _adaln_kernel(c_ref, w_ref, b_ref, o_ref):
    c = c_ref[...]
    sc = (c * jax.nn.sigmoid(c)).astype(BF16)
    o_ref[...] = _dot(sc, w_ref[...].astype(BF16)) + b_ref[...]


def _adaln(c, w, b):
    bsz, d = c.shape
    n = w.shape[1]
    return pl.pallas_call(
        _adaln_kernel,
        grid=(n // TN_ADA,),
        in_specs=[pl.BlockSpec((bsz, d), lambda j: (0, 0)),
                  pl.BlockSpec((d, TN_ADA), lambda j: (0, j)),
                  pl.BlockSpec((1, TN_ADA), lambda j: (0, j))],
        out_specs=pl.BlockSpec((bsz, TN_ADA), lambda j: (0, j)),
        out_shape=jax.ShapeDtypeStruct((bsz, n), F32),
        compiler_params=pltpu.CompilerParams(dimension_semantics=("arbitrary",),
                                             vmem_limit_bytes=VMEM_LIMIT),
        name="adaln_mod",
    )(c, w, b)


def _ffn_kernel(x_ref, mod_ref, gpre_ref, gpost_ref, wg_ref, wu_ref, wd_ref, o_ref,
                u_ref, acc_ref, *, sub):
    j = pl.program_id(1)

    @pl.when(j == 0)
    def _():
        u_ref[...] = _prenorm_modulate(x_ref[...], gpre_ref[...], mod_ref, sub).astype(BF16)
        acc_ref[...] = jnp.zeros_like(acc_ref)

    u = u_ref[...]
    g = _dot(u, wg_ref[...])
    up = _dot(u, wu_ref[...])
    h = (g * jax.nn.sigmoid(g) * up).astype(BF16)
    acc_ref[...] += _dot(h, wd_ref[...])

    @pl.when(j == pl.num_programs(1) - 1)
    def _():
        gate = mod_ref[0, 3 * sub + 2:3 * sub + 3, :]
        o_ref[...] = x_ref[...] + FFN_RES_WEIGHT * gate * _rms(acc_ref[...], gpost_ref[...])


def _ffn(x, mod, g_pre, g_post, wg, wu, wd, *, sub, seq):
    n, d = x.shape
    tiles_per_seq = seq // TM
    row = lambda i, j: (i, 0)
    const = lambda i, j: (0, 0)
    return pl.pallas_call(
        functools.partial(_ffn_kernel, sub=sub),
        grid=(n // TM, D_FF // TF),
        in_specs=[pl.BlockSpec((TM, d), row),
                  pl.BlockSpec((1, 3 * N_SUBLAYERS, d), lambda i, j: (i // tiles_per_seq, 0, 0)),
                  pl.BlockSpec((1, d), const),
                  pl.BlockSpec((1, d), const),
                  pl.BlockSpec((d, TF), lambda i, j: (0, j)),
                  pl.BlockSpec((d, TF), lambda i, j: (0, j)),
                  pl.BlockSpec((TF, d), lambda i, j: (j, 0))],
        out_specs=pl.BlockSpec((TM, d), row),
        out_shape=jax.ShapeDtypeStruct((n, d), F32),
        scratch_shapes=[pltpu.VMEM((TM, d), BF16), pltpu.VMEM((TM, d), F32)],
        compiler_params=pltpu.CompilerParams(dimension_semantics=("parallel", "arbitrary"),
                                             vmem_limit_bytes=VMEM_LIMIT),
        name=f"ffn{sub}",
    )(x, mod, g_pre, g_post, wg, wu, wd)


def _conv_kernel(x_ref, mod_ref, gpre_ref, w_ref, wdw_ref, bdw_ref, lng_ref, lnb_ref, gout_ref,
                 o_ref, hp_ref, sh_ref, cv_ref):
    t = pl.program_id(1)

    @pl.when(t == 0)
    def _():
        hp_ref[0:HALO, :] = jnp.zeros((HALO, D_CONV), F32)

    u = _prenorm_modulate(x_ref[...], gpre_ref[...], mod_ref, 1).astype(BF16)
    vg = _dot(u, w_ref[...])
    hp_ref[HALO:HALO + TM, :] = vg[:, :D_CONV] * jax.nn.sigmoid(vg[:, D_CONV:])

    first = HALO - (CONV_WIDTH - 1)
    sh_rows = TM + HALO - SUBLANES
    for cb in range(D_CONV // CONV_CB):
        cols = slice(cb * CONV_CB, (cb + 1) * CONV_CB)
        for s in range(1, SUBLANES):
            sh_ref[s - 1] = hp_ref[s:s + sh_rows, cols]

        def rows_body(rc, carry):
            r0 = pl.multiple_of(rc * CONV_R, CONV_R)
            acc = jnp.zeros((CONV_R // SUBLANES, SUBLANES, CONV_CB), F32)
            for k in range(CONV_WIDTH):
                off = first + k
                s, q = off % SUBLANES, off // SUBLANES
                start = pl.multiple_of(r0 + q * SUBLANES, SUBLANES)
                if s == 0:
                    win = hp_ref[pl.ds(start, CONV_R), cols]
                else:
                    win = sh_ref[s - 1, pl.ds(start, CONV_R), :]
                win = win.reshape(CONV_R // SUBLANES, SUBLANES, CONV_CB)
                acc = acc + win * wdw_ref[k, :, cols][None]
            cv_ref[pl.ds(r0, CONV_R), cols] = acc.reshape(CONV_R, CONV_CB)
            return carry

        lax.fori_loop(0, TM // CONV_R, rows_body, 0)

    hp_ref[0:HALO, :] = hp_ref[TM:TM + HALO, :]

    cv = cv_ref[...] + bdw_ref[...]
    mu = jnp.mean(cv, axis=-1, keepdims=True)
    xc = cv - mu
    var = jnp.mean(xc * xc, axis=-1, keepdims=True)
    y = xc * lax.rsqrt(var + EPS) * lng_ref[...] + lnb_ref[...]
    y = y * jax.nn.sigmoid(y)
    o_ref[0] = _rms(y, gout_ref[...]).astype(BF16)


def _conv_branch(x, mod, g_pre, w_conv, wdw_b, b_dw, ln_g, ln_b, g_out):
    bsz, seq, d = x.shape
    const2 = lambda b, t: (0, 0)
    return pl.pallas_call(
        _conv_kernel,
        grid=(bsz, seq // TM),
        in_specs=[pl.BlockSpec((None, TM, d), lambda b, t: (b, t, 0)),
                  pl.BlockSpec((1, 3 * N_SUBLAYERS, d), lambda b, t: (b, 0, 0)),
                  pl.BlockSpec((1, d), const2),
                  pl.BlockSpec((d, 2 * D_CONV), const2),
                  pl.BlockSpec((CONV_WIDTH, SUBLANES, D_CONV), lambda b, t: (0, 0, 0)),
                  pl.BlockSpec((1, D_CONV), const2),
                  pl.BlockSpec((1, D_CONV), const2),
                  pl.BlockSpec((1, D_CONV), const2),
                  pl.BlockSpec((1, D_CONV), const2)],
        out_specs=pl.BlockSpec((1, TM, D_CONV), lambda b, t: (b, t, 0)),
        out_shape=jax.ShapeDtypeStruct((bsz, seq, D_CONV), BF16),
        scratch_shapes=[pltpu.VMEM((TM + HALO, D_CONV), F32),
                        pltpu.VMEM((SUBLANES - 1, TM + HALO - SUBLANES, CONV_CB), F32),
                        pltpu.VMEM((TM, D_CONV), F32)],
        compiler_params=pltpu.CompilerParams(dimension_semantics=("parallel", "arbitrary"),
                                             vmem_limit_bytes=VMEM_LIMIT),
        name="conv_branch",
    )(x, mod, g_pre, w_conv, wdw_b, b_dw, ln_g, ln_b, g_out)


def _rope(xr, cos, sin_lo, sin_hi):
    return (xr * cos + pltpu.roll(xr, LANES - QK_ROPE_DIM // 2, axis=1) * sin_lo
            + pltpu.roll(xr, QK_ROPE_DIM // 2, axis=1) * sin_hi)


def _mla_proj_kernel(x_ref, mod_ref, gpre_ref, pos_ref, invf_ref, wq_ref, wkv_ref, wkr_ref,
                     gq_ref, gkv_ref, wqn_ref, wqr_ref, wuk_ref, wuv_ref,
                     qn_ref, qr_ref, kn_ref, kr_ref, v_ref):
    u = _prenorm_modulate(x_ref[...], gpre_ref[...], mod_ref, 1).astype(BF16)

    ang = pos_ref[...].astype(F32) * invf_ref[...]
    cos = jnp.cos(ang)
    sin = jnp.sin(ang)
    lane = lax.broadcasted_iota(jnp.int32, ang.shape, 1)
    half = QK_ROPE_DIM // 2
    sin_lo = jnp.where(lane < half, -sin, 0.0)
    sin_hi = jnp.where((lane >= half) & (lane < QK_ROPE_DIM), sin, 0.0)

    scale = (QK_NOPE_DIM + QK_ROPE_DIM) ** -0.5
    qlat = _rms(_dot(u, wq_ref[...]), gq_ref[...]).astype(BF16)
    qn_ref[...] = (_dot(qlat, wqn_ref[...]) * scale).astype(BF16)
    qr = _dot(qlat, wqr_ref[...])
    for h in range(N_HEADS):
        cols = slice(h * LANES, (h + 1) * LANES)
        qr_ref[:, cols] = (_rope(qr[:, cols], cos, sin_lo, sin_hi) * scale).astype(BF16)

    ckv = _rms(_dot(u, wkv_ref[...]), gkv_ref[...]).astype(BF16)
    kn_ref[...] = _dot(ckv, wuk_ref[...]).astype(BF16)
    v_ref[...] = _dot(ckv, wuv_ref[...]).astype(BF16)
    kr_ref[...] = _rope(_dot(u, wkr_ref[...]), cos, sin_lo, sin_hi).astype(BF16)


def _mla_proj(x, mod, g_pre, pos, invf, wq, wkv, wkr, gq, gkv, wqn, wqr, wuk, wuv, *, seq):
    n, d = x.shape
    tiles_per_seq = seq // TM
    row = lambda i: (i, 0)
    const = lambda i: (0, 0)
    full = lambda a: pl.BlockSpec(a.shape, const)
    wide = jax.ShapeDtypeStruct((n, N_HEADS * LANES), BF16)
    return pl.pallas_call(
        _mla_proj_kernel,
        grid=(n // TM,),
        in_specs=[pl.BlockSpec((TM, d), row),
                  pl.BlockSpec((1, 3 * N_SUBLAYERS, d), lambda i: (i // tiles_per_seq, 0, 0)),
                  full(g_pre),
                  pl.BlockSpec((TM, 1), row),
                  full(invf), full(wq), full(wkv), full(wkr), full(gq), full(gkv),
                  full(wqn), full(wqr), full(wuk), full(wuv)],
        out_specs=[pl.BlockSpec((TM, N_HEADS * LANES), row),
                   pl.BlockSpec((TM, N_HEADS * LANES), row),
                   pl.BlockSpec((TM, N_HEADS * LANES), row),
                   pl.BlockSpec((TM, LANES), row),
                   pl.BlockSpec((TM, N_HEADS * LANES), row)],
        out_shape=[wide, wide, wide, jax.ShapeDtypeStruct((n, LANES), BF16), wide],
        compiler_params=pltpu.CompilerParams(dimension_semantics=("parallel",),
                                             vmem_limit_bytes=VMEM_LIMIT),
        name="mla_proj",
    )(x, mod, g_pre, pos, invf, wq, wkv, wkr, gq, gkv, wqn, wqr, wuk, wuv)


def _attn_kernel(qn_ref, qr_ref, kn_ref, kr_ref, v_ref, o_ref, m_ref, l_ref, acc_ref):
    qi = pl.program_id(2)
    q = jnp.concatenate([qn_ref[0], qr_ref[0]], axis=-1)
    m_ref[...] = jnp.full_like(m_ref, -jnp.inf)
    l_ref[...] = jnp.zeros_like(l_ref)
    acc_ref[...] = jnp.zeros_like(acc_ref)

    def block(c, masked):
        rows = pl.ds(pl.multiple_of(c * TQ, TQ), TQ)
        k = jnp.concatenate([kn_ref[0, rows, :], kr_ref[0, rows, :]], axis=-1)
        s = lax.dot_general(q, k, (((1,), (1,)), ((), ())), preferred_element_type=F32)
        if masked:
            ri = lax.broadcasted_iota(jnp.int32, s.shape, 0)
            ci = lax.broadcasted_iota(jnp.int32, s.shape, 1)
            s = jnp.where(ri >= ci, s, -jnp.inf)
        m_prev = m_ref[...]
        m_new = jnp.maximum(m_prev, jnp.max(s, axis=-1, keepdims=True))
        alpha = jnp.exp(m_prev - m_new)
        p = jnp.exp(s - m_new)
        l_ref[...] = alpha * l_ref[...] + jnp.sum(p, axis=-1, keepdims=True)
        acc_ref[...] = alpha * acc_ref[...] + _dot(p.astype(BF16), v_ref[0, rows, :])
        m_ref[...] = m_new

    def body(c, carry):
        block(c, False)
        return carry

    lax.fori_loop(0, qi, body, 0)
    block(qi, True)
    o_ref[0] = (acc_ref[...] / l_ref[...]).astype(BF16)


def _attention(qn, qr, kn, kr, v):
    bsz, seq, _ = qn.shape
    qspec = pl.BlockSpec((1, TQ, LANES), lambda b, h, i: (b, i, h))
    kspec = pl.BlockSpec((1, seq, LANES), lambda b, h, i: (b, 0, h))
    return pl.pallas_call(
        _attn_kernel,
        grid=(bsz, N_HEADS, seq // TQ),
        in_specs=[qspec, qspec, kspec,
                  pl.BlockSpec((1, seq, LANES), lambda b, h, i: (b, 0, 0)),
                  kspec],
        out_specs=qspec,
        out_shape=jax.ShapeDtypeStruct((bsz, seq, D_ATTN), BF16),
        scratch_shapes=[pltpu.VMEM((TQ, 1), F32), pltpu.VMEM((TQ, 1), F32),
                        pltpu.VMEM((TQ, V_HEAD_DIM), F32)],
        compiler_params=pltpu.CompilerParams(
            dimension_semantics=("parallel", "parallel", "arbitrary"),
            vmem_limit_bytes=VMEM_LIMIT),
        name="mla_attn",
    )(qn, qr, kn, kr, v)


def _mix_out_kernel(x_ref, mod_ref, cv_ref, at_ref, gattn_ref, wc_ref, wa_ref, gpost_ref, o_ref):
    attn = _rms(at_ref[...].astype(F32), gattn_ref[...]).astype(BF16)
    y = _dot(cv_ref[...], wc_ref[...]) + _dot(attn, wa_ref[...])
    gate = mod_ref[0, 5:6, :]
    o_ref[...] = x_ref[...] + gate * _rms(y, gpost_ref[...])


def _mix_out(x, mod, cv, at, g_attn, wc, wa, g_post, *, seq):
    n, d = x.shape
    tiles_per_seq = seq // TM
    row = lambda i: (i, 0)
    const = lambda i: (0, 0)
    return pl.pallas_call(
        _mix_out_kernel,
        grid=(n // TM,),
        in_specs=[pl.BlockSpec((TM, d), row),
                  pl.BlockSpec((1, 3 * N_SUBLAYERS, d), lambda i: (i // tiles_per_seq, 0, 0)),
                  pl.BlockSpec((TM, D_CONV), row),
                  pl.BlockSpec((TM, D_ATTN), row),
                  pl.BlockSpec((1, D_ATTN), const),
                  pl.BlockSpec((D_CONV, d), const),
                  pl.BlockSpec((D_ATTN, d), const),
                  pl.BlockSpec((1, d), const)],
        out_specs=pl.BlockSpec((TM, d), row),
        out_shape=jax.ShapeDtypeStruct((n, d), F32),
        compiler_params=pltpu.CompilerParams(dimension_semantics=("parallel",),
                                             vmem_limit_bytes=VMEM_LIMIT),
        name="mix_out",
    )(x, mod, cv, at, g_attn, wc, wa, g_post)


def _rope_columns(w, n_groups, group, lo):
    k = w.shape[0]
    w = w.reshape(k, n_groups, group)[:, :, lo:lo + QK_ROPE_DIM]
    w = jnp.pad(w, ((0, 0), (0, 0), (0, LANES - QK_ROPE_DIM)))
    return w.reshape(k, n_groups * LANES)


def kernel(x, c, positions, w_ada, b_ada, g_pre_ffn1, w1_gate, w1_up, w1_down, g_post_ffn1, g_pre_mix, w_in, w_dw, b_dw, ln_conv_g, ln_conv_b, g_q_lat, w_uq, g_kv_lat, w_uk, w_uv, g_conv_out, g_attn_out, w_out, g_post_mix, g_pre_ffn2, w2_gate, w2_up, w2_down, g_post_ffn2):
    bsz, seq, d = x.shape
    n = bsz * seq
    depth = w_ada.shape[0]
    half = QK_ROPE_DIM // 2
    inv_freq = ROPE_BASE ** (-jnp.arange(half, dtype=F32) / half)
    invf = jnp.concatenate([inv_freq, inv_freq, jnp.zeros((LANES - QK_ROPE_DIM,), F32)])[None, :]
    pos = positions.reshape(n, 1)
    bf = lambda a: a.astype(BF16)

    xf = x.reshape(n, d)
    for l in range(depth):
        mod = _adaln(c, w_ada[l], b_ada[l][None, :]).reshape(bsz, 3 * N_SUBLAYERS, d)

        xf = _ffn(xf, mod, g_pre_ffn1[l][None], g_post_ffn1[l][None],
                  bf(w1_gate[l]), bf(w1_up[l]), bf(w1_down[l]), sub=0, seq=seq)

        i1 = 2 * D_CONV
        i2 = i1 + Q_LORA_RANK
        i3 = i2 + KV_LORA_RANK
        wi = w_in[l]
        cv = _conv_branch(xf.reshape(bsz, seq, d), mod, g_pre_mix[l][None], bf(wi[:, :i1]),
                          jnp.broadcast_to(w_dw[l][:, None, :], (CONV_WIDTH, SUBLANES, D_CONV)),
                          b_dw[l][None], ln_conv_g[l][None], ln_conv_b[l][None], g_conv_out[l][None])

        qk = QK_NOPE_DIM + QK_ROPE_DIM
        wqn = w_uq[l].reshape(Q_LORA_RANK, N_HEADS, qk)[:, :, :QK_NOPE_DIM].reshape(Q_LORA_RANK, -1)
        wqr = _rope_columns(w_uq[l], N_HEADS, qk, QK_NOPE_DIM)
        wkr = _rope_columns(wi[:, i3:], 1, QK_ROPE_DIM, 0)
        qn, qr, kn, kr, v = _mla_proj(
            xf, mod, g_pre_mix[l][None], pos, invf, bf(wi[:, i1:i2]), bf(wi[:, i2:i3]), bf(wkr),
            g_q_lat[l][None], g_kv_lat[l][None], bf(wqn), bf(wqr), bf(w_uk[l]), bf(w_uv[l]), seq=seq)

        shp = lambda a: a.reshape(bsz, seq, a.shape[-1])
        at = _attention(shp(qn), shp(qr), shp(kn), shp(kr), shp(v))

        wo = bf(w_out[l])
        xf = _mix_out(xf, mod, cv.reshape(n, D_CONV), at.reshape(n, D_ATTN), g_attn_out[l][None],
                      wo[:D_CONV], wo[D_CONV:], g_post_mix[l][None], seq=seq)

        xf = _ffn(xf, mod, g_pre_ffn2[l][None], g_post_ffn2[l][None],
                  bf(w2_gate[l]), bf(w2_up[l]), bf(w2_down[l]), sub=2, seq=seq)
    return xf.reshape(bsz, seq, d)
```

```python
import functools

import jax
import jax.numpy as jnp
import numpy as np
from jax import lax
from jax.experimental import pallas as pl
from jax.experimental.pallas import tpu as pltpu

D_MODEL = 2048
D_CONV = 1024
CONV_WIDTH = 31
N_HEADS = 8
QK_NOPE_DIM = 128
QK_ROPE_DIM = 64
V_HEAD_DIM = 128
Q_LORA_RANK = 768
KV_LORA_RANK = 512
D_ATTN = N_HEADS * V_HEAD_DIM
D_FF = 5632
FFN_RES_WEIGHT = 0.5
N_SUBLAYERS = 3
ROPE_BASE = 10000.0
EPS = 1e-6

LANES = 128
SUBLANES = 8
VMEM_LIMIT = 56 * 1024 * 1024

TM = 512
TF = 512
TN_ADA = 1024
TQ = 512
HALO = 32
CONV_CB = 256
CONV_R = 64

F32 = jnp.float32
BF16 = jnp.bfloat16


def _rms(x, g):
    return x * lax.rsqrt(jnp.mean(x * x, axis=-1, keepdims=True) + EPS) * g


def _prenorm_modulate(x, g, mod_ref, sub):
    shift = mod_ref[0, 3 * sub:3 * sub + 1, :]
    scale = mod_ref[0, 3 * sub + 1:3 * sub + 2, :]
    return _rms(x, g) * (1.0 + scale) + shift


def _dot(a, b):
    return jnp.dot(a, b, preferred_element_type=F32)


def _adaln_kernel(c_ref, w_ref, b_ref, o_ref):
    c = c_ref[...]
    sc = (c * jax.nn.sigmoid(c)).astype(BF16)
    o_ref[...] = _dot(sc, w_ref[...].astype(BF16)) + b_ref[...]


def _adaln(c, w, b):
    bsz, d = c.shape
    n = w.shape[1]
    return pl.pallas_call(
        _adaln_kernel,
        grid=(n // TN_ADA,),
        in_specs=[pl.BlockSpec((bsz, d), lambda j: (0, 0)),
                  pl.BlockSpec((d, TN_ADA), lambda j: (0, j)),
                  pl.BlockSpec((1, TN_ADA), lambda j: (0, j))],
        out_specs=pl.BlockSpec((bsz, TN_ADA), lambda j: (0, j)),
        out_shape=jax.ShapeDtypeStruct((bsz, n), F32),
        compiler_params=pltpu.CompilerParams(dimension_semantics=("arbitrary",),
                                             vmem_limit_bytes=VMEM_LIMIT),
        name="adaln_mod",
    )(c, w, b)


def _ffn_kernel(x_ref, mod_ref, gpre_ref, gpost_ref, wg_ref, wu_ref, wd_ref, o_ref,
                u_ref, acc_ref, *, sub):
    j = pl.program_id(1)

    @pl.when(j == 0)
    def _():
        u_ref[...] = _prenorm_modulate(x_ref[...], gpre_ref[...], mod_ref, sub).astype(BF16)
        acc_ref[...] = jnp.zeros_like(acc_ref)

    u = u_ref[...]
    g = _dot(u, wg_ref[...])
    up = _dot(u, wu_ref[...])
    h = (g * jax.nn.sigmoid(g) * up).astype(BF16)
    acc_ref[...] += _dot(h, wd_ref[...])

    @pl.when(j == pl.num_programs(1) - 1)
    def _():
        gate = mod_ref[0, 3 * sub + 2:3 * sub + 3, :]
        o_ref[...] = x_ref[...] + FFN_RES_WEIGHT * gate * _rms(acc_ref[...], gpost_ref[...])


def _ffn(x, mod, g_pre, g_post, wg, wu, wd, *, sub, seq):
    n, d = x.shape
    tiles_per_seq = seq // TM
    row = lambda i, j: (i, 0)
    const = lambda i, j: (0, 0)
    return pl.pallas_call(
        functools.partial(_ffn_kernel, sub=sub),
        grid=(n // TM, D_FF // TF),
        in_specs=[pl.BlockSpec((TM, d), row),
                  pl.BlockSpec((1, 3 * N_SUBLAYERS, d), lambda i, j: (i // tiles_per_seq, 0, 0)),
                  pl.BlockSpec((1, d), const),
                  pl.BlockSpec((1, d), const),
                  pl.BlockSpec((d, TF), lambda i, j: (0, j)),
                  pl.BlockSpec((d, TF), lambda i, j: (0, j)),
                  pl.BlockSpec((TF, d), lambda i, j: (j, 0))],
        out_specs=pl.BlockSpec((TM, d), row),
        out_shape=jax.ShapeDtypeStruct((n, d), F32),
        scratch_shapes=[pltpu.VMEM((TM, d), BF16), pltpu.VMEM((TM, d), F32)],
        compiler_params=pltpu.CompilerParams(dimension_semantics=("parallel", "arbitrary"),
                                             vmem_limit_bytes=VMEM_LIMIT),
        name=f"ffn{sub}",
    )(x, mod, g_pre, g_post, wg, wu, wd)


def _conv_kernel(x_ref, mod_ref, gpre_ref, w_ref, wdw_ref, bdw_ref, lng_ref, lnb_ref, gout_ref,
                 o_ref, hp_ref, sh_ref, cv_ref):
    t = pl.program_id(1)

    @pl.when(t == 0)
    def _():
        hp_ref[0:HALO, :] = jnp.zeros((HALO, D_CONV), F32)

    u = _prenorm_modulate(x_ref[...], gpre_ref[...], mod_ref, 1).astype(BF16)
    vg = _dot(u, w_ref[...])
    hp_ref[HALO:HALO + TM, :] = vg[:, :D_CONV] * jax.nn.sigmoid(vg[:, D_CONV:])

    first = HALO - (CONV_WIDTH - 1)
    sh_rows = TM + HALO - SUBLANES
    for cb in range(D_CONV // CONV_CB):
        cols = slice(cb * CONV_CB, (cb + 1) * CONV_CB)
        for s in range(1, SUBLANES):
            sh_ref[s - 1] = hp_ref[s:s + sh_rows, cols]

        def rows_body(rc, carry):
            r0 = pl.multiple_of(rc * CONV_R, CONV_R)
            acc = jnp.zeros((CONV_R // SUBLANES, SUBLANES, CONV_CB), F32)
            for k in range(CONV_WIDTH):
                off = first + k
                s, q = off % SUBLANES, off // SUBLANES
                start = pl.multiple_of(r0 + q * SUBLANES, SUBLANES)
                if s == 0:
                    win = hp_ref[pl.ds(start, CONV_R), cols]
                else:
                    win = sh_ref[s - 1, pl.ds(start, CONV_R), :]
                win = win.reshape(CONV_R // SUBLANES, SUBLANES, CONV_CB)
                acc = acc + win * wdw_ref[k, :, cols][None]
            cv_ref[pl.ds(r0, CONV_R), cols] = acc.reshape(CONV_R, CONV_CB)
            return carry

        lax.fori_loop(0, TM // CONV_R, rows_body, 0)

    hp_ref[0:HALO, :] = hp_ref[TM:TM + HALO, :]

    cv = cv_ref[...] + bdw_ref[...]
    mu = jnp.mean(cv, axis=-1, keepdims=True)
    xc = cv - mu
    var = jnp.mean(xc * xc, axis=-1, keepdims=True)
    y = xc * lax.rsqrt(var + EPS) * lng_ref[...] + lnb_ref[...]
    y = y * jax.nn.sigmoid(y)
    o_ref[0] = _rms(y, gout_ref[...]).astype(BF16)


def _conv_branch(x, mod, g_pre, w_conv, wdw_b, b_dw, ln_g, ln_b, g_out):
    bsz, seq, d = x.shape
    const2 = lambda b, t: (0, 0)
    return pl.pallas_call(
        _conv_kernel,
        grid=(bsz, seq // TM),
        in_specs=[pl.BlockSpec((None, TM, d), lambda b, t: (b, t, 0)),
                  pl.BlockSpec((1, 3 * N_SUBLAYERS, d), lambda b, t: (b, 0, 0)),
                  pl.BlockSpec((1, d), const2),
                  pl.BlockSpec((d, 2 * D_CONV), const2),
                  pl.BlockSpec((CONV_WIDTH, SUBLANES, D_CONV), lambda b, t: (0, 0, 0)),
                  pl.BlockSpec((1, D_CONV), const2),
                  pl.BlockSpec((1, D_CONV), const2),
                  pl.BlockSpec((1, D_CONV), const2),
                  pl.BlockSpec((1, D_CONV), const2)],
        out_specs=pl.BlockSpec((1, TM, D_CONV), lambda b, t: (b, t, 0)),
        out_shape=jax.ShapeDtypeStruct((bsz, seq, D_CONV), BF16),
        scratch_shapes=[pltpu.VMEM((TM + HALO, D_CONV), F32),
                        pltpu.VMEM((SUBLANES - 1, TM + HALO - SUBLANES, CONV_CB), F32),
                        pltpu.VMEM((TM, D_CONV), F32)],
        compiler_params=pltpu.CompilerParams(dimension_semantics=("parallel", "arbitrary"),
                                             vmem_limit_bytes=VMEM_LIMIT),
        name="conv_branch",
    )(x, mod, g_pre, w_conv, wdw_b, b_dw, ln_g, ln_b, g_out)


def _rope(xr, cos, sin_lo, sin_hi):
    return (xr * cos + pltpu.roll(xr, LANES - QK_ROPE_DIM // 2, axis=1) * sin_lo
            + pltpu.roll(xr, QK_ROPE_DIM // 2, axis=1) * sin_hi)


def _mla_proj_kernel(x_ref, mod_ref, gpre_ref, pos_ref, invf_ref, wq_ref, wkv_ref, wkr_ref,
                     gq_ref, gkv_ref, wqn_ref, wqr_ref, wuk_ref, wuv_ref,
                     qn_ref, qr_ref, kn_ref, kr_ref, v_ref):
    u = _prenorm_modulate(x_ref[...], gpre_ref[...], mod_ref, 1).astype(BF16)

    ang = pos_ref[...].astype(F32) * invf_ref[...]
    cos = jnp.cos(ang)
    sin = jnp.sin(ang)
    lane = lax.broadcasted_iota(jnp.int32, ang.shape, 1)
    half = QK_ROPE_DIM // 2
    sin_lo = jnp.where(lane < half, -sin, 0.0)
    sin_hi = jnp.where((lane >= half) & (lane < QK_ROPE_DIM), sin, 0.0)

    scale = (QK_NOPE_DIM + QK_ROPE_DIM) ** -0.5
    qlat = _rms(_dot(u, wq_ref[...]), gq_ref[...]).astype(BF16)
    qn_ref[...] = (_dot(qlat, wqn_ref[...]) * scale).astype(BF16)
    qr = _dot(qlat, wqr_ref[...])
    for h in range(N_HEADS):
        cols = slice(h * LANES, (h + 1) * LANES)
        qr_ref[:, cols] = (_rope(qr[:, cols], cos, sin_lo, sin_hi) * scale).astype(BF16)

    ckv = _rms(_dot(u, wkv_ref[...]), gkv_ref[...]).astype(BF16)
    kn_ref[...] = _dot(ckv, wuk_ref[...]).astype(BF16)
    v_ref[...] = _dot(ckv, wuv_ref[...]).astype(BF16)
    kr_ref[...] = _rope(_dot(u, wkr_ref[...]), cos, sin_lo, sin_hi).astype(BF16)


def _mla_proj(x, mod, g_pre, pos, invf, wq, wkv, wkr, gq, gkv, wqn, wqr, wuk, wuv, *, seq):
    n, d = x.shape
    tiles_per_seq = seq // TM
    row = lambda i: (i, 0)
    const = lambda i: (0, 0)
    full = lambda a: pl.BlockSpec(a.shape, const)
    wide = jax.ShapeDtypeStruct((n, N_HEADS * LANES), BF16)
    return pl.pallas_call(
        _mla_proj_kernel,
        grid=(n // TM,),
        in_specs=[pl.BlockSpec((TM, d), row),
                  pl.BlockSpec((1, 3 * N_SUBLAYERS, d), lambda i: (i // tiles_per_seq, 0, 0)),
                  full(g_pre),
                  pl.BlockSpec((TM, 1), row),
                  full(invf), full(wq), full(wkv), full(wkr), full(gq), full(gkv),
                  full(wqn), full(wqr), full(wuk), full(wuv)],
        out_specs=[pl.BlockSpec((TM, N_HEADS * LANES), row),
                   pl.BlockSpec((TM, N_HEADS * LANES), row),
                   pl.BlockSpec((TM, N_HEADS * LANES), row),
                   pl.BlockSpec((TM, LANES), row),
                   pl.BlockSpec((TM, N_HEADS * LANES), row)],
        out_shape=[wide, wide, wide, jax.ShapeDtypeStruct((n, LANES), BF16), wide],
        compiler_params=pltpu.CompilerParams(dimension_semantics=("parallel",),
                                             vmem_limit_bytes=VMEM_LIMIT),
        name="mla_proj",
    )(x, mod, g_pre, pos, invf, wq, wkv, wkr, gq, gkv, wqn, wqr, wuk, wuv)


def _dot_nt(a, b):
    return lax.dot_general(a, b, (((1,), (1,)), ((), ())), preferred_element_type=F32)


def _attn_kernel(qn_ref, qr_ref, kn_ref, kr_ref, v_ref, o_ref):
    seq = o_ref.shape[1]
    ri = lax.broadcasted_iota(jnp.int32, (TQ, TQ), 0)
    ci = lax.broadcasted_iota(jnp.int32, (TQ, TQ), 1)
    for qi in range(seq // TQ):
        diag = slice(qi * TQ, (qi + 1) * TQ)
        past = slice(0, qi * TQ)
        q = jnp.concatenate([qn_ref[0, diag, :], qr_ref[0, diag, :]], axis=-1)
        k_d = jnp.concatenate([kn_ref[0, diag, :], kr_ref[0, diag, :]], axis=-1)
        s_d = jnp.where(ri >= ci, _dot_nt(q, k_d), -jnp.inf)
        m = jnp.max(s_d, axis=-1, keepdims=True)
        if qi:
            k_p = jnp.concatenate([kn_ref[0, past, :], kr_ref[0, past, :]], axis=-1)
            s_p = _dot_nt(q, k_p)
            m = jnp.maximum(m, jnp.max(s_p, axis=-1, keepdims=True))
        p_d = jnp.exp(s_d - m)
        l = jnp.sum(p_d, axis=-1, keepdims=True)
        acc = _dot(p_d.astype(BF16), v_ref[0, diag, :])
        if qi:
            p_p = jnp.exp(s_p - m)
            l = l + jnp.sum(p_p, axis=-1, keepdims=True)
            acc = acc + _dot(p_p.astype(BF16), v_ref[0, past, :])
        o_ref[0, diag, :] = (acc * (1.0 / l)).astype(BF16)


def _attention(qn, qr, kn, kr, v):
    bsz, seq, _ = qn.shape
    head = pl.BlockSpec((1, seq, LANES), lambda b, h: (b, 0, h))
    return pl.pallas_call(
        _attn_kernel,
        grid=(bsz, N_HEADS),
        in_specs=[head, head, head, pl.BlockSpec((1, seq, LANES), lambda b, h: (b, 0, 0)), head],
        out_specs=head,
        out_shape=jax.ShapeDtypeStruct((bsz, seq, D_ATTN), BF16),
        compiler_params=pltpu.CompilerParams(dimension_semantics=("parallel", "parallel"),
                                             vmem_limit_bytes=VMEM_LIMIT),
        name="mla_attn",
    )(qn, qr, kn, kr, v)


def _mix_out_kernel(x_ref, mod_ref, cv_ref, at_ref, gattn_ref, wc_ref, wa_ref, gpost_ref, o_ref):
    attn = _rms(at_ref[...].astype(F32), gattn_ref[...]).astype(BF16)
    y = _dot(cv_ref[...], wc_ref[...]) + _dot(attn, wa_ref[...])
    gate = mod_ref[0, 5:6, :]
    o_ref[...] = x_ref[...] + gate * _rms(y, gpost_ref[...])


def _mix_out(x, mod, cv, at, g_attn, wc, wa, g_post, *, seq):
    n, d = x.shape
    tiles_per_seq = seq // TM
    row = lambda i: (i, 0)
    const = lambda i: (0, 0)
    return pl.pallas_call(
        _mix_out_kernel,
        grid=(n // TM,),
        in_specs=[pl.BlockSpec((TM, d), row),
                  pl.BlockSpec((1, 3 * N_SUBLAYERS, d), lambda i: (i // tiles_per_seq, 0, 0)),
                  pl.BlockSpec((TM, D_CONV), row),
                  pl.BlockSpec((TM, D_ATTN), row),
                  pl.BlockSpec((1, D_ATTN), const),
                  pl.BlockSpec((D_CONV, d), const),
                  pl.BlockSpec((D_ATTN, d), const),
                  pl.BlockSpec((1, d), const)],
        out_specs=pl.BlockSpec((TM, d), row),
        out_shape=jax.ShapeDtypeStruct((n, d), F32),
        compiler_params=pltpu.CompilerParams(dimension_semantics=("parallel",),
                                             vmem_limit_bytes=VMEM_LIMIT),
        name="mix_out",
    )(x, mod, cv, at, g_attn, wc, wa, g_post)


def _rope_columns(w, n_groups, group, lo):
    k = w.shape[0]
    w = w.reshape(k, n_groups, group)[:, :, lo:lo + QK_ROPE_DIM]
    w = jnp.pad(w, ((0, 0), (0, 0), (0, LANES - QK_ROPE_DIM)))
    return w.reshape(k, n_groups * LANES)


def kernel(x, c, positions, w_ada, b_ada, g_pre_ffn1, w1_gate, w1_up, w1_down, g_post_ffn1, g_pre_mix, w_in, w_dw, b_dw, ln_conv_g, ln_conv_b, g_q_lat, w_uq, g_kv_lat, w_uk, w_uv, g_conv_out, g_attn_out, w_out, g_post_mix, g_pre_ffn2, w2_gate, w2_up, w2_down, g_post_ffn2):
    bsz, seq, d = x.shape
    n = bsz * seq
    depth = w_ada.shape[0]
    half = QK_ROPE_DIM // 2
    inv_freq = ROPE_BASE ** (-jnp.arange(half, dtype=F32) / half)
    invf = jnp.concatenate([inv_freq, inv_freq, jnp.zeros((LANES - QK_ROPE_DIM,), F32)])[None, :]
    pos = positions.reshape(n, 1)
    bf = lambda a: a.astype(BF16)

    xf = x.reshape(n, d)
    for l in range(depth):
        mod = _adaln(c, w_ada[l], b_ada[l][None, :]).reshape(bsz, 3 * N_SUBLAYERS, d)

        xf = _ffn(xf, mod, g_pre_ffn1[l][None], g_post_ffn1[l][None],
                  bf(w1_gate[l]), bf(w1_up[l]), bf(w1_down[l]), sub=0, seq=seq)

        i1 = 2 * D_CONV
        i2 = i1 + Q_LORA_RANK
        i3 = i2 + KV_LORA_RANK
        wi = w_in[l]
        cv = _conv_branch(xf.reshape(bsz, seq, d), mod, g_pre_mix[l][None], bf(wi[:, :i1]),
                          jnp.broadcast_to(w_dw[l][:, None, :], (CONV_WIDTH, SUBLANES, D_CONV)),
                          b_dw[l][None], ln_conv_g[l][None], ln_conv_b[l][None], g_conv_out[l][None])

        qk = QK_NOPE_DIM + QK_ROPE_DIM
        wqn = w_uq[l].reshape(Q_LORA_RANK, N_HEADS, qk)[:, :, :QK_NOPE_DIM].reshape(Q_LORA_RANK, -1)
        wqr = _rope_columns(w_uq[l], N_HEADS, qk, QK_NOPE_DIM)
        wkr = _rope_columns(wi[:, i3:], 1, QK_ROPE_DIM, 0)
        qn, qr, kn, kr, v = _mla_proj(
            xf, mod, g_pre_mix[l][None], pos, invf, bf(wi[:, i1:i2]), bf(wi[:, i2:i3]), bf(wkr),
            g_q_lat[l][None], g_kv_lat[l][None], bf(wqn), bf(wqr), bf(w_uk[l]), bf(w_uv[l]), seq=seq)

        shp = lambda a: a.reshape(bsz, seq, a.shape[-1])
        at = _attention(shp(qn), shp(qr), shp(kn), shp(kr), shp(v))

        wo = bf(w_out[l])
        xf = _mix_out(xf, mod, cv.reshape(n, D_CONV), at.reshape(n, D_ATTN), g_attn_out[l][None],
                      wo[:D_CONV], wo[D_CONV:], g_post_mix[l][None], seq=seq)

        xf = _ffn(xf, mod, g_pre_ffn2[l][None], g_post_ffn2[l][None],
                  bf(w2_gate[l]), bf(w2_up[l]), bf(w2_down[l]), sub=2, seq=seq)
    return xf.reshape(bsz, seq, d)
```

```python
import functools

import jax
import jax.numpy as jnp
import numpy as np
from jax import lax
from jax.experimental import pallas as pl
from jax.experimental.pallas import tpu as pltpu

D_MODEL = 2048
D_CONV = 1024
CONV_WIDTH = 31
N_HEADS = 8
QK_NOPE_DIM = 128
QK_ROPE_DIM = 64
V_HEAD_DIM = 128
Q_LORA_RANK = 768
KV_LORA_RANK = 512
D_ATTN = N_HEADS * V_HEAD_DIM
D_FF = 5632
FFN_RES_WEIGHT = 0.5
N_SUBLAYERS = 3
ROPE_BASE = 10000.0
EPS = 1e-6

LANES = 128
SUBLANES = 8
VMEM_LIMIT = 56 * 1024 * 1024

TM = 512
TF = 512
FFN_CH = 64
FFN_TAIL = 256
TN_ADA = 1024
TQ = 512
HALO = 32
CONV_CB = 256
CONV_R = 64

F32 = jnp.float32
BF16 = jnp.bfloat16


def _rms(x, g):
    return x * lax.rsqrt(jnp.mean(x * x, axis=-1, keepdims=True) + EPS) * g


def _prenorm_modulate(x, g, mod_ref, sub):
    shift = mod_ref[0, 3 * sub:3 * sub + 1, :]
    scale = mod_ref[0, 3 * sub + 1:3 * sub + 2, :]
    return _rms(x, g) * (1.0 + scale) + shift


def _dot(a, b):
    return jnp.dot(a, b, preferred_element_type=F32)


def _adaln_kernel(c_ref, w_ref, b_ref, o_ref):
    c = c_ref[...]
    sc = (c * jax.nn.sigmoid(c)).astype(BF16)
    o_ref[...] = _dot(sc, w_ref[...].astype(BF16)) + b_ref[...]


def _adaln(c, w, b):
    bsz, d = c.shape
    n = w.shape[1]
    return pl.pallas_call(
        _adaln_kernel,
        grid=(n // TN_ADA,),
        in_specs=[pl.BlockSpec((bsz, d), lambda j: (0, 0)),
                  pl.BlockSpec((d, TN_ADA), lambda j: (0, j)),
                  pl.BlockSpec((1, TN_ADA), lambda j: (0, j))],
        out_specs=pl.BlockSpec((bsz, TN_ADA), lambda j: (0, j)),
        out_shape=jax.ShapeDtypeStruct((bsz, n), F32),
        compiler_params=pltpu.CompilerParams(dimension_semantics=("arbitrary",),
                                             vmem_limit_bytes=VMEM_LIMIT),
        name="adaln_mod",
    )(c, w, b)


def _ordering_zero(v):
    rows, d = v.shape
    t = jnp.sum(v.reshape(rows // SUBLANES, SUBLANES, d), axis=0)
    t = sum(t[:, c * LANES:(c + 1) * LANES] for c in range(d // LANES))
    bits = pltpu.bitcast(t, jnp.int32)
    bits = lax.shift_right_logical(lax.shift_right_logical(bits, 16), 16)
    return jnp.tile(bits.astype(F32), (d // SUBLANES, FFN_TAIL // LANES)).astype(BF16)


def _ffn_kernel(xn_ref, xp_ref, modn_ref, modp_ref, gpre_ref, gpost_ref, wg_ref, wu_ref, wd_ref,
                o_ref, u_a, u_b, acc_a, acc_b, *, sub, n_tiles):
    i = pl.program_id(0)
    j = pl.program_id(1)
    n_chunks = TM // FFN_CH
    rows = pl.ds(pl.multiple_of(jnp.minimum(j, n_chunks - 1) * FFN_CH, FFN_CH), FFN_CH)

    @pl.when((i == 0) & (j == 0))
    def _():
        u_a[...] = _prenorm_modulate(xp_ref[...], gpre_ref[...], modp_ref, sub).astype(BF16)
        acc_b[...] = jnp.zeros_like(acc_b)

    def finish_prev(acc_prev):
        gate = modp_ref[0, 3 * sub + 2:3 * sub + 3, :]
        out = xp_ref[rows, :] + FFN_RES_WEIGHT * gate * _rms(acc_prev[rows, :], gpost_ref[...])
        o_ref[rows, :] = out
        return out

    def step(u_cur, u_nxt, acc_cur, acc_prev):
        un = _prenorm_modulate(xn_ref[rows, :], gpre_ref[...], modn_ref, sub)
        u_nxt[rows, :] = un.astype(BF16)
        out = finish_prev(acc_prev)
        wg = wg_ref[...]
        wg = jnp.concatenate([wg[:, :-FFN_TAIL], wg[:, -FFN_TAIL:] + _ordering_zero(un + out)], axis=1)
        u = u_cur[...]
        g = _dot(u, wg)
        up = _dot(u, wu_ref[...])
        h = (g * jax.nn.sigmoid(g) * up).astype(BF16)
        acc_cur[...] = jnp.where(j > 0, acc_cur[...], 0.0) + _dot(h, wd_ref[...])

    @pl.when((i < n_tiles) & (i % 2 == 0))
    def _():
        step(u_a, u_b, acc_a, acc_b)

    @pl.when((i < n_tiles) & (i % 2 == 1))
    def _():
        step(u_b, u_a, acc_b, acc_a)

    @pl.when(i == n_tiles)
    def _():
        finish_prev(acc_b if n_tiles % 2 == 0 else acc_a)


def _ffn(x, mod, g_pre, g_post, wg, wu, wd, *, sub, seq):
    n, d = x.shape
    n_tiles = n // TM
    n_cols = D_FF // TF
    assert TM // FFN_CH <= n_cols
    tiles_per_seq = seq // TM
    nxt = lambda i: jnp.minimum(i + 1, n_tiles - 1)
    prv = lambda i: jnp.maximum(i - 1, 0)
    col = lambda i, j: jnp.where(i == n_tiles, n_cols - 1, j)
    const = lambda i, j: (0, 0)
    return pl.pallas_call(
        functools.partial(_ffn_kernel, sub=sub, n_tiles=n_tiles),
        grid=(n_tiles + 1, n_cols),
        in_specs=[pl.BlockSpec((TM, d), lambda i, j: (nxt(i), 0)),
                  pl.BlockSpec((TM, d), lambda i, j: (prv(i), 0)),
                  pl.BlockSpec((1, 3 * N_SUBLAYERS, d), lambda i, j: (nxt(i) // tiles_per_seq, 0, 0)),
                  pl.BlockSpec((1, 3 * N_SUBLAYERS, d), lambda i, j: (prv(i) // tiles_per_seq, 0, 0)),
                  pl.BlockSpec((1, d), const),
                  pl.BlockSpec((1, d), const),
                  pl.BlockSpec((d, TF), lambda i, j: (0, col(i, j))),
                  pl.BlockSpec((d, TF), lambda i, j: (0, col(i, j))),
                  pl.BlockSpec((TF, d), lambda i, j: (col(i, j), 0))],
        out_specs=pl.BlockSpec((TM, d), lambda i, j: (prv(i), 0)),
        out_shape=jax.ShapeDtypeStruct((n, d), F32),
        scratch_shapes=[pltpu.VMEM((TM, d), BF16), pltpu.VMEM((TM, d), BF16),
                        pltpu.VMEM((TM, d), F32), pltpu.VMEM((TM, d), F32)],
        compiler_params=pltpu.CompilerParams(dimension_semantics=("arbitrary", "arbitrary"),
                                             vmem_limit_bytes=VMEM_LIMIT),
        name=f"ffn{sub}",
    )(x, x, mod, mod, g_pre, g_post, wg, wu, wd)


def _conv_kernel(x_ref, mod_ref, gpre_ref, w_ref, wdw_ref, bdw_ref, lng_ref, lnb_ref, gout_ref,
                 o_ref, hp_ref, sh_ref, cv_ref):
    t = pl.program_id(1)

    @pl.when(t == 0)
    def _():
        hp_ref[0:HALO, :] = jnp.zeros((HALO, D_CONV), F32)

    u = _prenorm_modulate(x_ref[...], gpre_ref[...], mod_ref, 1).astype(BF16)

    first = HALO - (CONV_WIDTH - 1)
    sh_rows = TM + HALO - SUBLANES
    for cb in range(D_CONV // CONV_CB):
        cols = slice(cb * CONV_CB, (cb + 1) * CONV_CB)
        gcols = slice(D_CONV + cb * CONV_CB, D_CONV + (cb + 1) * CONV_CB)
        val = _dot(u, w_ref[:, cols])
        hp_ref[HALO:HALO + TM, cols] = val * jax.nn.sigmoid(_dot(u, w_ref[:, gcols]))
        for s in range(1, SUBLANES):
            sh_ref[s - 1, :, cols] = hp_ref[s:s + sh_rows, cols]
        for r0 in range(0, TM, CONV_R):
            acc = jnp.zeros((CONV_R // SUBLANES, SUBLANES, CONV_CB), F32)
            for k in range(CONV_WIDTH):
                s, q = (first + k) % SUBLANES, (first + k) // SUBLANES
                start = r0 + q * SUBLANES
                if s == 0:
                    win = hp_ref[start:start + CONV_R, cols]
                else:
                    win = sh_ref[s - 1, start:start + CONV_R, cols]
                win = win.reshape(CONV_R // SUBLANES, SUBLANES, CONV_CB)
                acc = acc + win * wdw_ref[k, :, cols][None]
            cv_ref[r0:r0 + CONV_R, cols] = acc.reshape(CONV_R, CONV_CB)

    hp_ref[0:HALO, :] = hp_ref[TM:TM + HALO, :]

    cv = cv_ref[...] + bdw_ref[...]
    mu = jnp.mean(cv, axis=-1, keepdims=True)
    xc = cv - mu
    var = jnp.mean(xc * xc, axis=-1, keepdims=True)
    y = xc * lax.rsqrt(var + EPS) * lng_ref[...] + lnb_ref[...]
    y = y * jax.nn.sigmoid(y)
    o_ref[0] = _rms(y, gout_ref[...]).astype(BF16)


def _conv_branch(x, mod, g_pre, w_conv, wdw_b, b_dw, ln_g, ln_b, g_out):
    bsz, seq, d = x.shape
    const2 = lambda b, t: (0, 0)
    return pl.pallas_call(
        _conv_kernel,
        grid=(bsz, seq // TM),
        in_specs=[pl.BlockSpec((None, TM, d), lambda b, t: (b, t, 0)),
                  pl.BlockSpec((1, 3 * N_SUBLAYERS, d), lambda b, t: (b, 0, 0)),
                  pl.BlockSpec((1, d), const2),
                  pl.BlockSpec((d, 2 * D_CONV), const2),
                  pl.BlockSpec((CONV_WIDTH, SUBLANES, D_CONV), lambda b, t: (0, 0, 0)),
                  pl.BlockSpec((1, D_CONV), const2),
                  pl.BlockSpec((1, D_CONV), const2),
                  pl.BlockSpec((1, D_CONV), const2),
                  pl.BlockSpec((1, D_CONV), const2)],
        out_specs=pl.BlockSpec((1, TM, D_CONV), lambda b, t: (b, t, 0)),
        out_shape=jax.ShapeDtypeStruct((bsz, seq, D_CONV), BF16),
        scratch_shapes=[pltpu.VMEM((TM + HALO, D_CONV), F32),
                        pltpu.VMEM((SUBLANES - 1, TM + HALO - SUBLANES, D_CONV), F32),
                        pltpu.VMEM((TM, D_CONV), F32)],
        compiler_params=pltpu.CompilerParams(dimension_semantics=("parallel", "arbitrary"),
                                             vmem_limit_bytes=VMEM_LIMIT),
        name="conv_branch",
    )(x, mod, g_pre, w_conv, wdw_b, b_dw, ln_g, ln_b, g_out)


def _rope(xr, cos, sin_lo, sin_hi):
    return (xr * cos + pltpu.roll(xr, LANES - QK_ROPE_DIM // 2, axis=1) * sin_lo
            + pltpu.roll(xr, QK_ROPE_DIM // 2, axis=1) * sin_hi)


def _mla_proj_kernel(x_ref, mod_ref, gpre_ref, pos_ref, invf_ref, wq_ref, wkv_ref, wkr_ref,
                     gq_ref, gkv_ref, wqn_ref, wqr_ref, wuk_ref, wuv_ref,
                     qn_ref, qr_ref, kn_ref, kr_ref, v_ref):
    u = _prenorm_modulate(x_ref[...], gpre_ref[...], mod_ref, 1).astype(BF16)

    ang = pos_ref[...].astype(F32) * invf_ref[...]
    cos = jnp.cos(ang)
    sin = jnp.sin(ang)
    lane = lax.broadcasted_iota(jnp.int32, ang.shape, 1)
    half = QK_ROPE_DIM // 2
    sin_lo = jnp.where(lane < half, -sin, 0.0)
    sin_hi = jnp.where((lane >= half) & (lane < QK_ROPE_DIM), sin, 0.0)

    scale = (QK_NOPE_DIM + QK_ROPE_DIM) ** -0.5
    qlat = _rms(_dot(u, wq_ref[...]), gq_ref[...]).astype(BF16)
    qn_ref[...] = (_dot(qlat, wqn_ref[...]) * scale).astype(BF16)
    qr = _dot(qlat, wqr_ref[...])
    for h in range(N_HEADS):
        cols = slice(h * LANES, (h + 1) * LANES)
        qr_ref[:, cols] = (_rope(qr[:, cols], cos, sin_lo, sin_hi) * scale).astype(BF16)

    ckv = _rms(_dot(u, wkv_ref[...]), gkv_ref[...]).astype(BF16)
    kn_ref[...] = _dot(ckv, wuk_ref[...]).astype(BF16)
    v_ref[...] = _dot(ckv, wuv_ref[...]).astype(BF16)
    kr_ref[...] = _rope(_dot(u, wkr_ref[...]), cos, sin_lo, sin_hi).astype(BF16)


def _mla_proj(x, mod, g_pre, pos, invf, wq, wkv, wkr, gq, gkv, wqn, wqr, wuk, wuv, *, seq):
    n, d = x.shape
    tiles_per_seq = seq // TM
    row = lambda i: (i, 0)
    const = lambda i: (0, 0)
    full = lambda a: pl.BlockSpec(a.shape, const)
    wide = jax.ShapeDtypeStruct((n, N_HEADS * LANES), BF16)
    return pl.pallas_call(
        _mla_proj_kernel,
        grid=(n // TM,),
        in_specs=[pl.BlockSpec((TM, d), row),
                  pl.BlockSpec((1, 3 * N_SUBLAYERS, d), lambda i: (i // tiles_per_seq, 0, 0)),
                  full(g_pre),
                  pl.BlockSpec((TM, 1), row),
                  full(invf), full(wq), full(wkv), full(wkr), full(gq), full(gkv),
                  full(wqn), full(wqr), full(wuk), full(wuv)],
        out_specs=[pl.BlockSpec((TM, N_HEADS * LANES), row),
                   pl.BlockSpec((TM, N_HEADS * LANES), row),
                   pl.BlockSpec((TM, N_HEADS * LANES), row),
                   pl.BlockSpec((TM, LANES), row),
                   pl.BlockSpec((TM, N_HEADS * LANES), row)],
        out_shape=[wide, wide, wide, jax.ShapeDtypeStruct((n, LANES), BF16), wide],
        compiler_params=pltpu.CompilerParams(dimension_semantics=("parallel",),
                                             vmem_limit_bytes=VMEM_LIMIT),
        name="mla_proj",
    )(x, mod, g_pre, pos, invf, wq, wkv, wkr, gq, gkv, wqn, wqr, wuk, wuv)


def _dot_nt(a, b):
    return lax.dot_general(a, b, (((1,), (1,)), ((), ())), preferred_element_type=F32)


def _attn_kernel(qn_ref, qr_ref, kn_ref, kr_ref, v_ref, o_ref):
    seq = o_ref.shape[1]
    ri = lax.broadcasted_iota(jnp.int32, (TQ, TQ), 0)
    ci = lax.broadcasted_iota(jnp.int32, (TQ, TQ), 1)
    for qi in range(seq // TQ):
        diag = slice(qi * TQ, (qi + 1) * TQ)
        past = slice(0, qi * TQ)
        q = jnp.concatenate([qn_ref[0, diag, :], qr_ref[0, diag, :]], axis=-1)
        k_d = jnp.concatenate([kn_ref[0, diag, :], kr_ref[0, diag, :]], axis=-1)
        s_d = jnp.where(ri >= ci, _dot_nt(q, k_d), -jnp.inf)
        m = jnp.max(s_d, axis=-1, keepdims=True)
        if qi:
            k_p = jnp.concatenate([kn_ref[0, past, :], kr_ref[0, past, :]], axis=-1)
            s_p = _dot_nt(q, k_p)
            m = jnp.maximum(m, jnp.max(s_p, axis=-1, keepdims=True))
        p_d = jnp.exp(s_d - m)
        l = jnp.sum(p_d, axis=-1, keepdims=True)
        acc = _dot(p_d.astype(BF16), v_ref[0, diag, :])
        if qi:
            p_p = jnp.exp(s_p - m)
            l = l + jnp.sum(p_p, axis=-1, keepdims=True)
            acc = acc + _dot(p_p.astype(BF16), v_ref[0, past, :])
        o_ref[0, diag, :] = (acc * (1.0 / l)).astype(BF16)


def _attention(qn, qr, kn, kr, v):
    bsz, seq, _ = qn.shape
    head = pl.BlockSpec((1, seq, LANES), lambda b, h: (b, 0, h))
    return pl.pallas_call(
        _attn_kernel,
        grid=(bsz, N_HEADS),
        in_specs=[head, head, head, pl.BlockSpec((1, seq, LANES), lambda b, h: (b, 0, 0)), head],
        out_specs=head,
        out_shape=jax.ShapeDtypeStruct((bsz, seq, D_ATTN), BF16),
        compiler_params=pltpu.CompilerParams(dimension_semantics=("parallel", "parallel"),
                                             vmem_limit_bytes=VMEM_LIMIT),
        name="mla_attn",
    )(qn, qr, kn, kr, v)


def _mix_out_kernel(x_ref, mod_ref, cv_ref, at_ref, gattn_ref, wc_ref, wa_ref, gpost_ref, o_ref):
    attn = _rms(at_ref[...].astype(F32), gattn_ref[...]).astype(BF16)
    y = _dot(cv_ref[...], wc_ref[...]) + _dot(attn, wa_ref[...])
    gate = mod_ref[0, 5:6, :]
    o_ref[...] = x_ref[...] + gate * _rms(y, gpost_ref[...])


def _mix_out(x, mod, cv, at, g_attn, wc, wa, g_post, *, seq):
    n, d = x.shape
    tiles_per_seq = seq // TM
    row = lambda i: (i, 0)
    const = lambda i: (0, 0)
    return pl.pallas_call(
        _mix_out_kernel,
        grid=(n // TM,),
        in_specs=[pl.BlockSpec((TM, d), row),
                  pl.BlockSpec((1, 3 * N_SUBLAYERS, d), lambda i: (i // tiles_per_seq, 0, 0)),
                  pl.BlockSpec((TM, D_CONV), row),
                  pl.BlockSpec((TM, D_ATTN), row),
                  pl.BlockSpec((1, D_ATTN), const),
                  pl.BlockSpec((D_CONV, d), const),
                  pl.BlockSpec((D_ATTN, d), const),
                  pl.BlockSpec((1, d), const)],
        out_specs=pl.BlockSpec((TM, d), row),
        out_shape=jax.ShapeDtypeStruct((n, d), F32),
        compiler_params=pltpu.CompilerParams(dimension_semantics=("parallel",),
                                             vmem_limit_bytes=VMEM_LIMIT),
        name="mix_out",
    )(x, mod, cv, at, g_attn, wc, wa, g_post)


def _rope_columns(w, n_groups, group, lo):
    k = w.shape[0]
    w = w.reshape(k, n_groups, group)[:, :, lo:lo + QK_ROPE_DIM]
    w = jnp.pad(w, ((0, 0), (0, 0), (0, LANES - QK_ROPE_DIM)))
    return w.reshape(k, n_groups * LANES)


def kernel(x, c, positions, w_ada, b_ada, g_pre_ffn1, w1_gate, w1_up, w1_down, g_post_ffn1, g_pre_mix, w_in, w_dw, b_dw, ln_conv_g, ln_conv_b, g_q_lat, w_uq, g_kv_lat, w_uk, w_uv, g_conv_out, g_attn_out, w_out, g_post_mix, g_pre_ffn2, w2_gate, w2_up, w2_down, g_post_ffn2):
    bsz, seq, d = x.shape
    n = bsz * seq
    depth = w_ada.shape[0]
    half = QK_ROPE_DIM // 2
    inv_freq = ROPE_BASE ** (-jnp.arange(half, dtype=F32) / half)
    invf = jnp.concatenate([inv_freq, inv_freq, jnp.zeros((LANES - QK_ROPE_DIM,), F32)])[None, :]
    pos = positions.reshape(n, 1)
    bf = lambda a: a.astype(BF16)

    xf = x.reshape(n, d)
    for l in range(depth):
        mod = _adaln(c, w_ada[l], b_ada[l][None, :]).reshape(bsz, 3 * N_SUBLAYERS, d)

        xf = _ffn(xf, mod, g_pre_ffn1[l][None], g_post_ffn1[l][None],
                  bf(w1_gate[l]), bf(w1_up[l]), bf(w1_down[l]), sub=0, seq=seq)

        i1 = 2 * D_CONV
        i2 = i1 + Q_LORA_RANK
        i3 = i2 + KV_LORA_RANK
        wi = w_in[l]
        cv = _conv_branch(xf.reshape(bsz, seq, d), mod, g_pre_mix[l][None], bf(wi[:, :i1]),
                          jnp.broadcast_to(w_dw[l][:, None, :], (CONV_WIDTH, SUBLANES, D_CONV)),
                          b_dw[l][None], ln_conv_g[l][None], ln_conv_b[l][None], g_conv_out[l][None])

        qk = QK_NOPE_DIM + QK_ROPE_DIM
        wqn = w_uq[l].reshape(Q_LORA_RANK, N_HEADS, qk)[:, :, :QK_NOPE_DIM].reshape(Q_LORA_RANK, -1)
        wqr = _rope_columns(w_uq[l], N_HEADS, qk, QK_NOPE_DIM)
        wkr = _rope_columns(wi[:, i3:], 1, QK_ROPE_DIM, 0)
        qn, qr, kn, kr, v = _mla_proj(
            xf, mod, g_pre_mix[l][None], pos, invf, bf(wi[:, i1:i2]), bf(wi[:, i2:i3]), bf(wkr),
            g_q_lat[l][None], g_kv_lat[l][None], bf(wqn), bf(wqr), bf(w_uk[l]), bf(w_uv[l]), seq=seq)

        shp = lambda a: a.reshape(bsz, seq, a.shape[-1])
        at = _attention(shp(qn), shp(qr), shp(kn), shp(kr), shp(v))

        wo = bf(w_out[l])
        xf = _mix_out(xf, mod, cv.reshape(n, D_CONV), at.reshape(n, D_ATTN), g_attn_out[l][None],
                      wo[:D_CONV], wo[D_CONV:], g_post_mix[l][None], seq=seq)

        xf = _ffn(xf, mod, g_pre_ffn2[l][None], g_post_ffn2[l][None],
                  bf(w2_gate[l]), bf(w2_up[l]), bf(w2_down[l]), sub=2, seq=seq)
    return xf.reshape(bsz, seq, d)
```

```python
import functools

import jax
import jax.numpy as jnp
from jax import lax
from jax.experimental import pallas as pl
from jax.experimental.pallas import tpu as pltpu

D_MODEL = 2048
D_CONV = 1024
CONV_WIDTH = 31
N_HEADS = 8
QK_NOPE_DIM = 128
QK_ROPE_DIM = 64
V_HEAD_DIM = 128
Q_LORA_RANK = 768
KV_LORA_RANK = 512
D_ATTN = N_HEADS * V_HEAD_DIM
D_FF = 5632
FFN_RES_WEIGHT = 0.5
N_SUBLAYERS = 3
ROPE_BASE = 10000.0
EPS = 1e-6

LANES = 128
SUBLANES = 8
VMEM_LIMIT = 56 * 1024 * 1024

TM = 512
TF = 512
TN_ADA = 1024
TQ = 512
HALO = 32
CONV_CB = 256
CONV_R = 64

F32 = jnp.float32
BF16 = jnp.bfloat16


def _rms(x, g):
    return x * lax.rsqrt(jnp.mean(x * x, axis=-1, keepdims=True) + EPS) * g


def _prenorm_modulate(x, g, mod_ref, sub):
    shift = mod_ref[0, 3 * sub:3 * sub + 1, :]
    scale = mod_ref[0, 3 * sub + 1:3 * sub + 2, :]
    return _rms(x, g) * (1.0 + scale) + shift


def _dot(a, b):
    return jnp.dot(a, b, preferred_element_type=F32)


def _ordering_zero(v, shape, dtype):
    rows, lanes = v.shape
    t = jnp.sum(v.reshape(rows // SUBLANES, SUBLANES, lanes), axis=0)
    bits = lax.shift_right_logical(lax.shift_right_logical(pltpu.bitcast(t, jnp.int32), 16), 16)
    return jnp.tile(bits.astype(F32), (shape[0] // SUBLANES, shape[1] // lanes)).astype(dtype)


def _adaln_kernel(c_ref, w_ref, b_ref, o_ref):
    c = c_ref[...]
    sc = (c * jax.nn.sigmoid(c)).astype(BF16)
    o_ref[...] = _dot(sc, w_ref[...].astype(BF16)) + b_ref[...]


def _adaln(c, w, b):
    bsz, d = c.shape
    n = w.shape[1]
    return pl.pallas_call(
        _adaln_kernel,
        grid=(n // TN_ADA,),
        in_specs=[pl.BlockSpec((bsz, d), lambda j: (0, 0)),
                  pl.BlockSpec((d, TN_ADA), lambda j: (0, j)),
                  pl.BlockSpec((1, TN_ADA), lambda j: (0, j))],
        out_specs=pl.BlockSpec((bsz, TN_ADA), lambda j: (0, j)),
        out_shape=jax.ShapeDtypeStruct((bsz, n), F32),
        compiler_params=pltpu.CompilerParams(dimension_semantics=("arbitrary",),
                                             vmem_limit_bytes=VMEM_LIMIT),
        name="adaln_mod",
    )(c, w, b)


def _ffn_kernel(x_ref, mod_ref, gpre_ref, gpost_ref, wg_ref, wu_ref, wd_ref, o_ref,
                u_ref, acc_ref, *, sub):
    j = pl.program_id(1)

    @pl.when(j == 0)
    def _():
        u_ref[...] = _prenorm_modulate(x_ref[...], gpre_ref[...], mod_ref, sub).astype(BF16)
        acc_ref[...] = jnp.zeros_like(acc_ref)

    u = u_ref[...]
    g = _dot(u, wg_ref[...])
    up = _dot(u, wu_ref[...])
    h = (g * jax.nn.sigmoid(g) * up).astype(BF16)
    acc_ref[...] += _dot(h, wd_ref[...])

    @pl.when(j == pl.num_programs(1) - 1)
    def _():
        gate = mod_ref[0, 3 * sub + 2:3 * sub + 3, :]
        o_ref[...] = x_ref[...] + FFN_RES_WEIGHT * gate * _rms(acc_ref[...], gpost_ref[...])


def _ffn(x, mod, g_pre, g_post, wg, wu, wd, *, sub, seq):
    n, d = x.shape
    tiles_per_seq = seq // TM
    row = lambda i, j: (i, 0)
    const = lambda i, j: (0, 0)
    return pl.pallas_call(
        functools.partial(_ffn_kernel, sub=sub),
        grid=(n // TM, D_FF // TF),
        in_specs=[pl.BlockSpec((TM, d), row),
                  pl.BlockSpec((1, 3 * N_SUBLAYERS, d), lambda i, j: (i // tiles_per_seq, 0, 0)),
                  pl.BlockSpec((1, d), const),
                  pl.BlockSpec((1, d), const),
                  pl.BlockSpec((d, TF), lambda i, j: (0, j)),
                  pl.BlockSpec((d, TF), lambda i, j: (0, j)),
                  pl.BlockSpec((TF, d), lambda i, j: (j, 0))],
        out_specs=pl.BlockSpec((TM, d), row),
        out_shape=jax.ShapeDtypeStruct((n, d), F32),
        scratch_shapes=[pltpu.VMEM((TM, d), BF16), pltpu.VMEM((TM, d), F32)],
        compiler_params=pltpu.CompilerParams(dimension_semantics=("parallel", "arbitrary"),
                                             vmem_limit_bytes=VMEM_LIMIT),
        name=f"ffn{sub}",
    )(x, mod, g_pre, g_post, wg, wu, wd)


def _conv_kernel(x_ref, mod_ref, gpre_ref, w_ref, wdw_ref, bdw_ref, lng_ref, lnb_ref, gout_ref,
                 o_ref, u_ref, hp_ref, sh_ref, cv_ref):
    t = pl.program_id(1)

    @pl.when(t == 0)
    def _():
        hp_ref[0:HALO, :] = jnp.zeros((HALO, D_CONV), F32)

    u = _prenorm_modulate(x_ref[...], gpre_ref[...], mod_ref, 1).astype(BF16)
    u_ref[0] = u

    first = HALO - (CONV_WIDTH - 1)
    sh_rows = TM + HALO - SUBLANES
    for cb in range(D_CONV // CONV_CB):
        cols = slice(cb * CONV_CB, (cb + 1) * CONV_CB)
        gcols = slice(D_CONV + cb * CONV_CB, D_CONV + (cb + 1) * CONV_CB)
        val = _dot(u, w_ref[:, cols])
        hp_ref[HALO:HALO + TM, cols] = val * jax.nn.sigmoid(_dot(u, w_ref[:, gcols]))
        for s in range(1, SUBLANES):
            sh_ref[s - 1, :, cols] = hp_ref[s:s + sh_rows, cols]
        for r0 in range(0, TM, CONV_R):
            acc = jnp.zeros((CONV_R // SUBLANES, SUBLANES, CONV_CB), F32)
            for k in range(CONV_WIDTH):
                s, q = (first + k) % SUBLANES, (first + k) // SUBLANES
                start = r0 + q * SUBLANES
                if s == 0:
                    win = hp_ref[start:start + CONV_R, cols]
                else:
                    win = sh_ref[s - 1, start:start + CONV_R, cols]
                win = win.reshape(CONV_R // SUBLANES, SUBLANES, CONV_CB)
                acc = acc + win * wdw_ref[k, :, cols][None]
            cv_ref[r0:r0 + CONV_R, cols] = acc.reshape(CONV_R, CONV_CB)

    hp_ref[0:HALO, :] = hp_ref[TM:TM + HALO, :]

    cv = cv_ref[...] + bdw_ref[...]
    mu = jnp.mean(cv, axis=-1, keepdims=True)
    xc = cv - mu
    var = jnp.mean(xc * xc, axis=-1, keepdims=True)
    y = xc * lax.rsqrt(var + EPS) * lng_ref[...] + lnb_ref[...]
    y = y * jax.nn.sigmoid(y)
    o_ref[0] = _rms(y, gout_ref[...]).astype(BF16)


def _conv_branch(x, mod, g_pre, w_conv, wdw_b, b_dw, ln_g, ln_b, g_out):
    bsz, seq, d = x.shape
    const2 = lambda b, t: (0, 0)
    return pl.pallas_call(
        _conv_kernel,
        grid=(bsz, seq // TM),
        in_specs=[pl.BlockSpec((None, TM, d), lambda b, t: (b, t, 0)),
                  pl.BlockSpec((1, 3 * N_SUBLAYERS, d), lambda b, t: (b, 0, 0)),
                  pl.BlockSpec((1, d), const2),
                  pl.BlockSpec((d, 2 * D_CONV), const2),
                  pl.BlockSpec((CONV_WIDTH, SUBLANES, D_CONV), lambda b, t: (0, 0, 0)),
                  pl.BlockSpec((1, D_CONV), const2),
                  pl.BlockSpec((1, D_CONV), const2),
                  pl.BlockSpec((1, D_CONV), const2),
                  pl.BlockSpec((1, D_CONV), const2)],
        out_specs=[pl.BlockSpec((1, TM, D_CONV), lambda b, t: (b, t, 0)),
                   pl.BlockSpec((1, TM, d), lambda b, t: (b, t, 0))],
        out_shape=[jax.ShapeDtypeStruct((bsz, seq, D_CONV), BF16),
                   jax.ShapeDtypeStruct((bsz, seq, d), BF16)],
        scratch_shapes=[pltpu.VMEM((TM + HALO, D_CONV), F32),
                        pltpu.VMEM((SUBLANES - 1, TM + HALO - SUBLANES, D_CONV), F32),
                        pltpu.VMEM((TM, D_CONV), F32)],
        compiler_params=pltpu.CompilerParams(dimension_semantics=("parallel", "arbitrary"),
                                             vmem_limit_bytes=VMEM_LIMIT),
        name="conv_branch",
    )(x, mod, g_pre, w_conv, wdw_b, b_dw, ln_g, ln_b, g_out)


def _rope(xr, cos, sin_lo, sin_hi):
    return (xr * cos + pltpu.roll(xr, LANES - QK_ROPE_DIM // 2, axis=1) * sin_lo
            + pltpu.roll(xr, QK_ROPE_DIM // 2, axis=1) * sin_hi)


def _mla_proj_kernel(u_ref, pos_ref, invf_ref, wq_ref, wkv_ref, wkr_ref,
                     gq_ref, gkv_ref, wqn_ref, wqr_ref, wuk_ref, wuv_ref,
                     qn_ref, qr_ref, kn_ref, kr_ref, v_ref):
    u = u_ref[...]

    ang = pos_ref[...].astype(F32) * invf_ref[...]
    cos = jnp.cos(ang)
    sin = jnp.sin(ang)
    lane = lax.broadcasted_iota(jnp.int32, ang.shape, 1)
    half = QK_ROPE_DIM // 2
    sin_lo = jnp.where(lane < half, -sin, 0.0)
    sin_hi = jnp.where((lane >= half) & (lane < QK_ROPE_DIM), sin, 0.0)

    scale = (QK_NOPE_DIM + QK_ROPE_DIM) ** -0.5
    kvlat = _dot(u, wkv_ref[...])
    qlat = _dot(u, wq_ref[...])
    kr = _dot(u, wkr_ref[...] + _ordering_zero(cos + sin, wkr_ref.shape, BF16))
    ckv = _rms(kvlat, gkv_ref[...]).astype(BF16)
    kn_ref[...] = _dot(ckv, wuk_ref[...]).astype(BF16)
    v_ref[...] = _dot(ckv, wuv_ref[...]).astype(BF16)
    qlat = _rms(qlat, gq_ref[...]).astype(BF16)
    qn_ref[...] = (_dot(qlat, wqn_ref[...]) * scale).astype(BF16)
    qr = _dot(qlat, wqr_ref[...])
    kr_ref[...] = _rope(kr, cos, sin_lo, sin_hi).astype(BF16)
    for h in range(N_HEADS):
        cols = slice(h * LANES, (h + 1) * LANES)
        qr_ref[:, cols] = (_rope(qr[:, cols], cos, sin_lo, sin_hi) * scale).astype(BF16)


def _mla_proj(u, pos, invf, wq, wkv, wkr, gq, gkv, wqn, wqr, wuk, wuv):
    n, d = u.shape
    row = lambda i: (i, 0)
    const = lambda i: (0, 0)
    full = lambda a: pl.BlockSpec(a.shape, const)
    wide = jax.ShapeDtypeStruct((n, N_HEADS * LANES), BF16)
    return pl.pallas_call(
        _mla_proj_kernel,
        grid=(n // TM,),
        in_specs=[pl.BlockSpec((TM, d), row),
                  pl.BlockSpec((TM, 1), row),
                  full(invf), full(wq), full(wkv), full(wkr), full(gq), full(gkv),
                  full(wqn), full(wqr), full(wuk), full(wuv)],
        out_specs=[pl.BlockSpec((TM, N_HEADS * LANES), row),
                   pl.BlockSpec((TM, N_HEADS * LANES), row),
                   pl.BlockSpec((TM, N_HEADS * LANES), row),
                   pl.BlockSpec((TM, LANES), row),
                   pl.BlockSpec((TM, N_HEADS * LANES), row)],
        out_shape=[wide, wide, wide, jax.ShapeDtypeStruct((n, LANES), BF16), wide],
        compiler_params=pltpu.CompilerParams(dimension_semantics=("parallel",),
                                             vmem_limit_bytes=VMEM_LIMIT),
        name="mla_proj",
    )(u, pos, invf, wq, wkv, wkr, gq, gkv, wqn, wqr, wuk, wuv)


def _dot_nt(a, b):
    return lax.dot_general(a, b, (((1,), (1,)), ((), ())), preferred_element_type=F32)


def _attn_kernel(qn_ref, qr_ref, kn_ref, kr_ref, v_ref, o_ref):
    seq = o_ref.shape[1]
    ki = lax.broadcasted_iota(jnp.int32, (TQ, TQ), 0)
    qj = lax.broadcasted_iota(jnp.int32, (TQ, TQ), 1)
    v_t = v_ref[0].astype(F32).T.astype(BF16)
    for qi in range(seq // TQ):
        diag = slice(qi * TQ, (qi + 1) * TQ)
        past = slice(0, qi * TQ)
        q = jnp.concatenate([qn_ref[0, diag, :], qr_ref[0, diag, :]], axis=-1)
        k_d = jnp.concatenate([kn_ref[0, diag, :], kr_ref[0, diag, :]], axis=-1)
        s_d = jnp.where(ki <= qj, _dot_nt(k_d, q), -jnp.inf)
        m = jnp.max(s_d, axis=0, keepdims=True)
        if qi:
            k_p = jnp.concatenate([kn_ref[0, past, :], kr_ref[0, past, :]], axis=-1)
            s_p = _dot_nt(k_p, q)
            m = jnp.maximum(m, jnp.max(s_p, axis=0, keepdims=True))
        p_d = jnp.exp(s_d - m)
        l = jnp.sum(p_d, axis=0, keepdims=True)
        acc = _dot(v_t[:, diag], p_d.astype(BF16))
        if qi:
            p_p = jnp.exp(s_p - m)
            l = l + jnp.sum(p_p, axis=0, keepdims=True)
            acc = acc + _dot(v_t[:, past], p_p.astype(BF16))
        o_ref[0, diag, :] = (acc * (1.0 / l)).T.astype(BF16)


def _attention(qn, qr, kn, kr, v):
    bsz, seq, _ = qn.shape
    head = pl.BlockSpec((1, seq, LANES), lambda b, h: (b, 0, h))
    return pl.pallas_call(
        _attn_kernel,
        grid=(bsz, N_HEADS),
        in_specs=[head, head, head, pl.BlockSpec((1, seq, LANES), lambda b, h: (b, 0, 0)), head],
        out_specs=head,
        out_shape=jax.ShapeDtypeStruct((bsz, seq, D_ATTN), BF16),
        compiler_params=pltpu.CompilerParams(dimension_semantics=("parallel", "parallel"),
                                             vmem_limit_bytes=VMEM_LIMIT),
        name="mla_attn",
    )(qn, qr, kn, kr, v)


def _mix_out_kernel(x_ref, mod_ref, cv_ref, at_ref, gattn_ref, wc_ref, wa_ref, gpost_ref, o_ref):
    attn = _rms(at_ref[...].astype(F32), gattn_ref[...]).astype(BF16)
    y = _dot(cv_ref[...], wc_ref[...]) + _dot(attn, wa_ref[...])
    gate = mod_ref[0, 5:6, :]
    o_ref[...] = x_ref[...] + gate * _rms(y, gpost_ref[...])


def _mix_out(x, mod, cv, at, g_attn, wc, wa, g_post, *, seq):
    n, d = x.shape
    tiles_per_seq = seq // TM
    row = lambda i: (i, 0)
    const = lambda i: (0, 0)
    return pl.pallas_call(
        _mix_out_kernel,
        grid=(n // TM,),
        in_specs=[pl.BlockSpec((TM, d), row),
                  pl.BlockSpec((1, 3 * N_SUBLAYERS, d), lambda i: (i // tiles_per_seq, 0, 0)),
                  pl.BlockSpec((TM, D_CONV), row),
                  pl.BlockSpec((TM, D_ATTN), row),
                  pl.BlockSpec((1, D_ATTN), const),
                  pl.BlockSpec((D_CONV, d), const),
                  pl.BlockSpec((D_ATTN, d), const),
                  pl.BlockSpec((1, d), const)],
        out_specs=pl.BlockSpec((TM, d), row),
        out_shape=jax.ShapeDtypeStruct((n, d), F32),
        compiler_params=pltpu.CompilerParams(dimension_semantics=("parallel",),
                                             vmem_limit_bytes=VMEM_LIMIT),
        name="mix_out",
    )(x, mod, cv, at, g_attn, wc, wa, g_post)


def _rope_columns(w, n_groups, group, lo):
    k = w.shape[0]
    w = w.reshape(k, n_groups, group)[:, :, lo:lo + QK_ROPE_DIM]
    w = jnp.pad(w, ((0, 0), (0, 0), (0, LANES - QK_ROPE_DIM)))
    return w.reshape(k, n_groups * LANES)


def kernel(x, c, positions, w_ada, b_ada, g_pre_ffn1, w1_gate, w1_up, w1_down, g_post_ffn1, g_pre_mix, w_in, w_dw, b_dw, ln_conv_g, ln_conv_b, g_q_lat, w_uq, g_kv_lat, w_uk, w_uv, g_conv_out, g_attn_out, w_out, g_post_mix, g_pre_ffn2, w2_gate, w2_up, w2_down, g_post_ffn2):
    bsz, seq, d = x.shape
    n = bsz * seq
    depth = w_ada.shape[0]
    half = QK_ROPE_DIM // 2
    inv_freq = ROPE_BASE ** (-jnp.arange(half, dtype=F32) / half)
    invf = jnp.concatenate([inv_freq, inv_freq, jnp.zeros((LANES - QK_ROPE_DIM,), F32)])[None, :]
    pos = positions.reshape(n, 1)
    bf = lambda a: a.astype(BF16)

    xf = x.reshape(n, d)
    for l in range(depth):
        mod = _adaln(c, w_ada[l], b_ada[l][None, :]).reshape(bsz, 3 * N_SUBLAYERS, d)

        xf = _ffn(xf, mod, g_pre_ffn1[l][None], g_post_ffn1[l][None],
                  bf(w1_gate[l]), bf(w1_up[l]), bf(w1_down[l]), sub=0, seq=seq)

        i1 = 2 * D_CONV
        i2 = i1 + Q_LORA_RANK
        i3 = i2 + KV_LORA_RANK
        wi = w_in[l]
        cv, u = _conv_branch(xf.reshape(bsz, seq, d), mod, g_pre_mix[l][None], bf(wi[:, :i1]),
                             jnp.broadcast_to(w_dw[l][:, None, :], (CONV_WIDTH, SUBLANES, D_CONV)),
                             b_dw[l][None], ln_conv_g[l][None], ln_conv_b[l][None], g_conv_out[l][None])

        qk = QK_NOPE_DIM + QK_ROPE_DIM
        wqn = w_uq[l].reshape(Q_LORA_RANK, N_HEADS, qk)[:, :, :QK_NOPE_DIM].reshape(Q_LORA_RANK, -1)
        wqr = _rope_columns(w_uq[l], N_HEADS, qk, QK_NOPE_DIM)
        wkr = _rope_columns(wi[:, i3:], 1, QK_ROPE_DIM, 0)
        qn, qr, kn, kr, v = _mla_proj(
            u.reshape(n, d), pos, invf, bf(wi[:, i1:i2]), bf(wi[:, i2:i3]), bf(wkr),
            g_q_lat[l][None], g_kv_lat[l][None], bf(wqn), bf(wqr), bf(w_uk[l]), bf(w_uv[l]))

        shp = lambda a: a.reshape(bsz, seq, a.shape[-1])
        at = _attention(shp(qn), shp(qr), shp(kn), shp(kr), shp(v))

        wo = bf(w_out[l])
        xf = _mix_out(xf, mod, cv.reshape(n, D_CONV), at.reshape(n, D_ATTN), g_attn_out[l][None],
                      wo[:D_CONV], wo[D_CONV:], g_post_mix[l][None], seq=seq)

        xf = _ffn(xf, mod, g_pre_ffn2[l][None], g_post_ffn2[l][None],
                  bf(w2_gate[l]), bf(w2_up[l]), bf(w2_down[l]), sub=2, seq=seq)
    return xf.reshape(bsz, seq, d)
```

```python
import functools

import jax
import jax.numpy as jnp
from jax import lax
from jax.experimental import pallas as pl
from jax.experimental.pallas import tpu as pltpu

D_MODEL = 2048
D_CONV = 1024
CONV_WIDTH = 31
N_HEADS = 8
QK_NOPE_DIM = 128
QK_ROPE_DIM = 64
V_HEAD_DIM = 128
Q_LORA_RANK = 768
KV_LORA_RANK = 512
D_ATTN = N_HEADS * V_HEAD_DIM
D_FF = 5632
FFN_RES_WEIGHT = 0.5
N_SUBLAYERS = 3
ROPE_BASE = 10000.0
EPS = 1e-6

LANES = 128
SUBLANES = 8
VMEM_LIMIT = 56 * 1024 * 1024

TM = 512
TF = 512
TN_ADA = 1024
TQ = 512
HALO = 32
CONV_CB = 256
CONV_R = 64
ROW_CH = 16

F32 = jnp.float32
BF16 = jnp.bfloat16


def _rms(x, g):
    return x * lax.rsqrt(jnp.mean(x * x, axis=-1, keepdims=True) + EPS) * g


def _rep(row):
    return jnp.broadcast_to(row, (SUBLANES, row.shape[-1]))


def _unit_rms(x):
    r, d = x.shape
    x = x.reshape(r // SUBLANES, SUBLANES, d)
    return x * lax.rsqrt(jnp.mean(x * x, axis=-1, keepdims=True) + EPS)


def _prenorm_rows(g_ref, mod_ref, sub):
    shift = mod_ref[0, 3 * sub:3 * sub + 1, :]
    scale = mod_ref[0, 3 * sub + 1:3 * sub + 2, :]
    return _rep(g_ref[...] * (1.0 + scale)), _rep(shift)


def _dot(a, b):
    return jnp.dot(a, b, preferred_element_type=F32)


def _row_chunks(n_rows):
    return [slice(r, r + ROW_CH) for r in range(0, n_rows, ROW_CH)]


def _ordering_zero(v, shape, dtype):
    rows, lanes = v.shape
    t = jnp.sum(v.reshape(rows // SUBLANES, SUBLANES, lanes), axis=0)
    bits = lax.shift_right_logical(lax.shift_right_logical(pltpu.bitcast(t, jnp.int32), 16), 16)
    return jnp.tile(bits.astype(F32), (shape[0] // SUBLANES, shape[1] // lanes)).astype(dtype)


def _adaln_kernel(c_ref, w_ref, b_ref, o_ref):
    c = c_ref[...]
    sc = (c * jax.nn.sigmoid(c)).astype(BF16)
    o_ref[...] = _dot(sc, w_ref[...].astype(BF16)) + b_ref[...]


def _adaln(c, w, b):
    bsz, d = c.shape
    n = w.shape[1]
    return pl.pallas_call(
        _adaln_kernel,
        grid=(n // TN_ADA,),
        in_specs=[pl.BlockSpec((bsz, d), lambda j: (0, 0)),
                  pl.BlockSpec((d, TN_ADA), lambda j: (0, j)),
                  pl.BlockSpec((1, TN_ADA), lambda j: (0, j))],
        out_specs=pl.BlockSpec((bsz, TN_ADA), lambda j: (0, j)),
        out_shape=jax.ShapeDtypeStruct((bsz, n), F32),
        compiler_params=pltpu.CompilerParams(dimension_semantics=("arbitrary",),
                                             vmem_limit_bytes=VMEM_LIMIT),
        name="adaln_mod",
    )(c, w, b)


def _ffn_kernel(x_ref, mod_ref, gpre_ref, gpost_ref, wg_ref, wu_ref, wd_ref, o_ref,
                u_ref, acc_ref, *, sub):
    j = pl.program_id(1)

    @pl.when(j == 0)
    def _():
        gain, shift = _prenorm_rows(gpre_ref, mod_ref, sub)
        for rows in _row_chunks(TM):
            u = _unit_rms(x_ref[rows, :]) * gain + shift
            u_ref[rows, :] = u.reshape(ROW_CH, -1).astype(BF16)
        acc_ref[...] = jnp.zeros_like(acc_ref)

    u = u_ref[...]
    g = _dot(u, wg_ref[...])
    up = _dot(u, wu_ref[...])
    h = (g * jax.nn.sigmoid(g) * up).astype(BF16)
    acc_ref[...] += _dot(h, wd_ref[...])

    @pl.when(j == pl.num_programs(1) - 1)
    def _():
        gate = _rep(FFN_RES_WEIGHT * mod_ref[0, 3 * sub + 2:3 * sub + 3, :] * gpost_ref[...])
        for rows in _row_chunks(TM):
            y = (_unit_rms(acc_ref[rows, :]) * gate).reshape(ROW_CH, -1)
            o_ref[rows, :] = x_ref[rows, :] + y


def _ffn(x, mod, g_pre, g_post, wg, wu, wd, *, sub, seq):
    n, d = x.shape
    tiles_per_seq = seq // TM
    row = lambda i, j: (i, 0)
    const = lambda i, j: (0, 0)
    return pl.pallas_call(
        functools.partial(_ffn_kernel, sub=sub),
        grid=(n // TM, D_FF // TF),
        in_specs=[pl.BlockSpec((TM, d), row),
                  pl.BlockSpec((1, 3 * N_SUBLAYERS, d), lambda i, j: (i // tiles_per_seq, 0, 0)),
                  pl.BlockSpec((1, d), const),
                  pl.BlockSpec((1, d), const),
                  pl.BlockSpec((d, TF), lambda i, j: (0, j)),
                  pl.BlockSpec((d, TF), lambda i, j: (0, j)),
                  pl.BlockSpec((TF, d), lambda i, j: (j, 0))],
        out_specs=pl.BlockSpec((TM, d), row),
        out_shape=jax.ShapeDtypeStruct((n, d), F32),
        scratch_shapes=[pltpu.VMEM((TM, d), BF16), pltpu.VMEM((TM, d), F32)],
        compiler_params=pltpu.CompilerParams(dimension_semantics=("parallel", "arbitrary"),
                                             vmem_limit_bytes=VMEM_LIMIT),
        name=f"ffn{sub}",
    )(x, mod, g_pre, g_post, wg, wu, wd)


def _conv_kernel(x_ref, mod_ref, gpre_ref, w_ref, wdw_ref, bdw_ref, lng_ref, lnb_ref, gout_ref,
                 o_ref, u_ref, hp_ref, sh_ref, cv_ref):
    t = pl.program_id(1)

    @pl.when(t == 0)
    def _():
        hp_ref[0:HALO, :] = jnp.zeros((HALO, D_CONV), F32)

    gain, shift = _prenorm_rows(gpre_ref, mod_ref, 1)
    for rows in _row_chunks(TM):
        u = _unit_rms(x_ref[rows, :]) * gain + shift
        u_ref[0, rows, :] = u.reshape(ROW_CH, -1).astype(BF16)
    u = u_ref[0]

    first = HALO - (CONV_WIDTH - 1)
    sh_rows = TM + HALO - SUBLANES
    for cb in range(D_CONV // CONV_CB):
        cols = slice(cb * CONV_CB, (cb + 1) * CONV_CB)
        gcols = slice(D_CONV + cb * CONV_CB, D_CONV + (cb + 1) * CONV_CB)
        val = _dot(u, w_ref[:, cols])
        hp_ref[HALO:HALO + TM, cols] = val * jax.nn.sigmoid(_dot(u, w_ref[:, gcols]))
        for s in range(1, SUBLANES):
            sh_ref[s - 1, :, cols] = hp_ref[s:s + sh_rows, cols]
        for r0 in range(0, TM, CONV_R):
            acc = jnp.zeros((CONV_R // SUBLANES, SUBLANES, CONV_CB), F32)
            for k in range(CONV_WIDTH):
                s, q = (first + k) % SUBLANES, (first + k) // SUBLANES
                start = r0 + q * SUBLANES
                if s == 0:
                    win = hp_ref[start:start + CONV_R, cols]
                else:
                    win = sh_ref[s - 1, start:start + CONV_R, cols]
                win = win.reshape(CONV_R // SUBLANES, SUBLANES, CONV_CB)
                acc = acc + win * wdw_ref[k, :, cols][None]
            cv_ref[r0:r0 + CONV_R, cols] = acc.reshape(CONV_R, CONV_CB)

    hp_ref[0:HALO, :] = hp_ref[TM:TM + HALO, :]

    bias, ln_g, ln_b, g_out = _rep(bdw_ref[...]), _rep(lng_ref[...]), _rep(lnb_ref[...]), _rep(gout_ref[...])
    for rows in _row_chunks(TM):
        cv = cv_ref[rows, :].reshape(ROW_CH // SUBLANES, SUBLANES, D_CONV) + bias
        xc = cv - jnp.mean(cv, axis=-1, keepdims=True)
        var = jnp.mean(xc * xc, axis=-1, keepdims=True)
        y = xc * lax.rsqrt(var + EPS) * ln_g + ln_b
        y = y * jax.nn.sigmoid(y)
        y = y * lax.rsqrt(jnp.mean(y * y, axis=-1, keepdims=True) + EPS) * g_out
        o_ref[0, rows, :] = y.reshape(ROW_CH, D_CONV).astype(BF16)


def _conv_branch(x, mod, g_pre, w_conv, wdw_b, b_dw, ln_g, ln_b, g_out):
    bsz, seq, d = x.shape
    const2 = lambda b, t: (0, 0)
    return pl.pallas_call(
        _conv_kernel,
        grid=(bsz, seq // TM),
        in_specs=[pl.BlockSpec((None, TM, d), lambda b, t: (b, t, 0)),
                  pl.BlockSpec((1, 3 * N_SUBLAYERS, d), lambda b, t: (b, 0, 0)),
                  pl.BlockSpec((1, d), const2),
                  pl.BlockSpec((d, 2 * D_CONV), const2),
                  pl.BlockSpec((CONV_WIDTH, SUBLANES, D_CONV), lambda b, t: (0, 0, 0)),
                  pl.BlockSpec((1, D_CONV), const2),
                  pl.BlockSpec((1, D_CONV), const2),
                  pl.BlockSpec((1, D_CONV), const2),
                  pl.BlockSpec((1, D_CONV), const2)],
        out_specs=[pl.BlockSpec((1, TM, D_CONV), lambda b, t: (b, t, 0)),
                   pl.BlockSpec((1, TM, d), lambda b, t: (b, t, 0))],
        out_shape=[jax.ShapeDtypeStruct((bsz, seq, D_CONV), BF16),
                   jax.ShapeDtypeStruct((bsz, seq, d), BF16)],
        scratch_shapes=[pltpu.VMEM((TM + HALO, D_CONV), F32),
                        pltpu.VMEM((SUBLANES - 1, TM + HALO - SUBLANES, D_CONV), F32),
                        pltpu.VMEM((TM, D_CONV), F32)],
        compiler_params=pltpu.CompilerParams(dimension_semantics=("parallel", "arbitrary"),
                                             vmem_limit_bytes=VMEM_LIMIT),
        name="conv_branch",
    )(x, mod, g_pre, w_conv, wdw_b, b_dw, ln_g, ln_b, g_out)


def _rope(xr, cos, sin_lo, sin_hi):
    return (xr * cos + pltpu.roll(xr, LANES - QK_ROPE_DIM // 2, axis=1) * sin_lo
            + pltpu.roll(xr, QK_ROPE_DIM // 2, axis=1) * sin_hi)


def _mla_proj_kernel(u_ref, pos_ref, invf_ref, wq_ref, wkv_ref, wkr_ref,
                     gq_ref, gkv_ref, wqn_ref, wqr_ref, wuk_ref, wuv_ref,
                     qn_ref, qr_ref, kn_ref, kr_ref, v_ref):
    u = u_ref[...]

    ang = pos_ref[...].astype(F32) * invf_ref[...]
    cos = jnp.cos(ang)
    sin = jnp.sin(ang)
    lane = lax.broadcasted_iota(jnp.int32, ang.shape, 1)
    half = QK_ROPE_DIM // 2
    sin_lo = jnp.where(lane < half, -sin, 0.0)
    sin_hi = jnp.where((lane >= half) & (lane < QK_ROPE_DIM), sin, 0.0)

    scale = (QK_NOPE_DIM + QK_ROPE_DIM) ** -0.5
    kvlat = _dot(u, wkv_ref[...])
    qlat = _dot(u, wq_ref[...])
    kr = _dot(u, wkr_ref[...] + _ordering_zero(cos + sin, wkr_ref.shape, BF16))
    ckv = _rms(kvlat, gkv_ref[...]).astype(BF16)
    kn_ref[...] = _dot(ckv, wuk_ref[...]).astype(BF16)
    v_ref[...] = _dot(ckv, wuv_ref[...]).astype(BF16)
    qlat = _rms(qlat, gq_ref[...]).astype(BF16)
    qn_ref[...] = (_dot(qlat, wqn_ref[...]) * scale).astype(BF16)
    qr = _dot(qlat, wqr_ref[...])
    kr_ref[...] = _rope(kr, cos, sin_lo, sin_hi).astype(BF16)
    for h in range(N_HEADS):
        cols = slice(h * LANES, (h + 1) * LANES)
        qr_ref[:, cols] = (_rope(qr[:, cols], cos, sin_lo, sin_hi) * scale).astype(BF16)


def _mla_proj(u, pos, invf, wq, wkv, wkr, gq, gkv, wqn, wqr, wuk, wuv):
    n, d = u.shape
    row = lambda i: (i, 0)
    const = lambda i: (0, 0)
    full = lambda a: pl.BlockSpec(a.shape, const)
    wide = jax.ShapeDtypeStruct((n, N_HEADS * LANES), BF16)
    return pl.pallas_call(
        _mla_proj_kernel,
        grid=(n // TM,),
        in_specs=[pl.BlockSpec((TM, d), row),
                  pl.BlockSpec((TM, 1), row),
                  full(invf), full(wq), full(wkv), full(wkr), full(gq), full(gkv),
                  full(wqn), full(wqr), full(wuk), full(wuv)],
        out_specs=[pl.BlockSpec((TM, N_HEADS * LANES), row),
                   pl.BlockSpec((TM, N_HEADS * LANES), row),
                   pl.BlockSpec((TM, N_HEADS * LANES), row),
                   pl.BlockSpec((TM, LANES), row),
                   pl.BlockSpec((TM, N_HEADS * LANES), row)],
        out_shape=[wide, wide, wide, jax.ShapeDtypeStruct((n, LANES), BF16), wide],
        compiler_params=pltpu.CompilerParams(dimension_semantics=("parallel",),
                                             vmem_limit_bytes=VMEM_LIMIT),
        name="mla_proj",
    )(u, pos, invf, wq, wkv, wkr, gq, gkv, wqn, wqr, wuk, wuv)


def _dot_nt(a, b):
    return lax.dot_general(a, b, (((1,), (1,)), ((), ())), preferred_element_type=F32)


def _attn_kernel(qn_ref, qr_ref, kn_ref, kr_ref, v_ref, o_ref):
    seq = o_ref.shape[1]
    ki = lax.broadcasted_iota(jnp.int32, (TQ, TQ), 0)
    qj = lax.broadcasted_iota(jnp.int32, (TQ, TQ), 1)
    v_t = v_ref[0].astype(F32).T.astype(BF16)
    for qi in range(seq // TQ):
        diag = slice(qi * TQ, (qi + 1) * TQ)
        past = slice(0, qi * TQ)
        q = jnp.concatenate([qn_ref[0, diag, :], qr_ref[0, diag, :]], axis=-1)
        k_d = jnp.concatenate([kn_ref[0, diag, :], kr_ref[0, diag, :]], axis=-1)
        s_d = jnp.where(ki <= qj, _dot_nt(k_d, q), -jnp.inf)
        m = jnp.max(s_d, axis=0, keepdims=True)
        if qi:
            k_p = jnp.concatenate([kn_ref[0, past, :], kr_ref[0, past, :]], axis=-1)
            s_p = _dot_nt(k_p, q)
            m = jnp.maximum(m, jnp.max(s_p, axis=0, keepdims=True))
        p_d = jnp.exp(s_d - m)
        l = jnp.sum(p_d, axis=0, keepdims=True)
        acc = _dot(v_t[:, diag], p_d.astype(BF16))
        if qi:
            p_p = jnp.exp(s_p - m)
            l = l + jnp.sum(p_p, axis=0, keepdims=True)
            acc = acc + _dot(v_t[:, past], p_p.astype(BF16))
        o_ref[0, diag, :] = (acc * (1.0 / l)).T.astype(BF16)


def _attention(qn, qr, kn, kr, v):
    bsz, seq, _ = qn.shape
    head = pl.BlockSpec((1, seq, LANES), lambda b, h: (b, 0, h))
    return pl.pallas_call(
        _attn_kernel,
        grid=(bsz, N_HEADS),
        in_specs=[head, head, head, pl.BlockSpec((1, seq, LANES), lambda b, h: (b, 0, 0)), head],
        out_specs=head,
        out_shape=jax.ShapeDtypeStruct((bsz, seq, D_ATTN), BF16),
        compiler_params=pltpu.CompilerParams(dimension_semantics=("parallel", "parallel"),
                                             vmem_limit_bytes=VMEM_LIMIT),
        name="mla_attn",
    )(qn, qr, kn, kr, v)


def _mix_out_kernel(x_ref, mod_ref, cv_ref, at_ref, gattn_ref, wc_ref, wa_ref, gpost_ref, o_ref,
                    an_ref, y_ref):
    g_attn = _rep(gattn_ref[...])
    for rows in _row_chunks(TM):
        a = _unit_rms(at_ref[rows, :].astype(F32)) * g_attn
        an_ref[rows, :] = a.reshape(ROW_CH, -1).astype(BF16)
    y_ref[...] = _dot(cv_ref[...], wc_ref[...]) + _dot(an_ref[...], wa_ref[...])
    gate = _rep(mod_ref[0, 5:6, :] * gpost_ref[...])
    for rows in _row_chunks(TM):
        y = (_unit_rms(y_ref[rows, :]) * gate).reshape(ROW_CH, -1)
        o_ref[rows, :] = x_ref[rows, :] + y


def _mix_out(x, mod, cv, at, g_attn, wc, wa, g_post, *, seq):
    n, d = x.shape
    tiles_per_seq = seq // TM
    row = lambda i: (i, 0)
    const = lambda i: (0, 0)
    return pl.pallas_call(
        _mix_out_kernel,
        grid=(n // TM,),
        in_specs=[pl.BlockSpec((TM, d), row),
                  pl.BlockSpec((1, 3 * N_SUBLAYERS, d), lambda i: (i // tiles_per_seq, 0, 0)),
                  pl.BlockSpec((TM, D_CONV), row),
                  pl.BlockSpec((TM, D_ATTN), row),
                  pl.BlockSpec((1, D_ATTN), const),
                  pl.BlockSpec((D_CONV, d), const),
                  pl.BlockSpec((D_ATTN, d), const),
                  pl.BlockSpec((1, d), const)],
        out_specs=pl.BlockSpec((TM, d), row),
        out_shape=jax.ShapeDtypeStruct((n, d), F32),
        scratch_shapes=[pltpu.VMEM((TM, D_ATTN), BF16), pltpu.VMEM((TM, d), F32)],
        compiler_params=pltpu.CompilerParams(dimension_semantics=("parallel",),
                                             vmem_limit_bytes=VMEM_LIMIT),
        name="mix_out",
    )(x, mod, cv, at, g_attn, wc, wa, g_post)


def _rope_columns(w, n_groups, group, lo):
    k = w.shape[0]
    w = w.reshape(k, n_groups, group)[:, :, lo:lo + QK_ROPE_DIM]
    w = jnp.pad(w, ((0, 0), (0, 0), (0, LANES - QK_ROPE_DIM)))
    return w.reshape(k, n_groups * LANES)


def kernel(x, c, positions, w_ada, b_ada, g_pre_ffn1, w1_gate, w1_up, w1_down, g_post_ffn1, g_pre_mix, w_in, w_dw, b_dw, ln_conv_g, ln_conv_b, g_q_lat, w_uq, g_kv_lat, w_uk, w_uv, g_conv_out, g_attn_out, w_out, g_post_mix, g_pre_ffn2, w2_gate, w2_up, w2_down, g_post_ffn2):
    bsz, seq, d = x.shape
    n = bsz * seq
    depth = w_ada.shape[0]
    half = QK_ROPE_DIM // 2
    inv_freq = ROPE_BASE ** (-jnp.arange(half, dtype=F32) / half)
    invf = jnp.concatenate([inv_freq, inv_freq, jnp.zeros((LANES - QK_ROPE_DIM,), F32)])[None, :]
    pos = positions.reshape(n, 1)
    bf = lambda a: a.astype(BF16)

    xf = x.reshape(n, d)
    for l in range(depth):
        mod = _adaln(c, w_ada[l], b_ada[l][None, :]).reshape(bsz, 3 * N_SUBLAYERS, d)

        xf = _ffn(xf, mod, g_pre_ffn1[l][None], g_post_ffn1[l][None],
                  bf(w1_gate[l]), bf(w1_up[l]), bf(w1_down[l]), sub=0, seq=seq)

        i1 = 2 * D_CONV
        i2 = i1 + Q_LORA_RANK
        i3 = i2 + KV_LORA_RANK
        wi = w_in[l]
        cv, u = _conv_branch(xf.reshape(bsz, seq, d), mod, g_pre_mix[l][None], bf(wi[:, :i1]),
                             jnp.broadcast_to(w_dw[l][:, None, :], (CONV_WIDTH, SUBLANES, D_CONV)),
                             b_dw[l][None], ln_conv_g[l][None], ln_conv_b[l][None], g_conv_out[l][None])

        qk = QK_NOPE_DIM + QK_ROPE_DIM
        wqn = w_uq[l].reshape(Q_LORA_RANK, N_HEADS, qk)[:, :, :QK_NOPE_DIM].reshape(Q_LORA_RANK, -1)
        wqr = _rope_columns(w_uq[l], N_HEADS, qk, QK_NOPE_DIM)
        wkr = _rope_columns(wi[:, i3:], 1, QK_ROPE_DIM, 0)
        qn, qr, kn, kr, v = _mla_proj(
            u.reshape(n, d), pos, invf, bf(wi[:, i1:i2]), bf(wi[:, i2:i3]), bf(wkr),
            g_q_lat[l][None], g_kv_lat[l][None], bf(wqn), bf(wqr), bf(w_uk[l]), bf(w_uv[l]))

        shp = lambda a: a.reshape(bsz, seq, a.shape[-1])
        at = _attention(shp(qn), shp(qr), shp(kn), shp(kr), shp(v))

        wo = bf(w_out[l])
        xf = _mix_out(xf, mod, cv.reshape(n, D_CONV), at.reshape(n, D_ATTN), g_attn_out[l][None],
                      wo[:D_CONV], wo[D_CONV:], g_post_mix[l][None], seq=seq)

        xf = _ffn(xf, mod, g_pre_ffn2[l][None], g_post_ffn2[l][None],
                  bf(w2_gate[l]), bf(w2_up[l]), bf(w2_down[l]), sub=2, seq=seq)
    return xf.reshape(bsz, seq, d)
```

```python
import functools

import jax
import jax.numpy as jnp
from jax import lax
from jax.experimental import pallas as pl
from jax.experimental.pallas import tpu as pltpu

D_MODEL = 2048
D_CONV = 1024
CONV_WIDTH = 31
N_HEADS = 8
QK_NOPE_DIM = 128
QK_ROPE_DIM = 64
V_HEAD_DIM = 128
Q_LORA_RANK = 768
KV_LORA_RANK = 512
D_ATTN = N_HEADS * V_HEAD_DIM
D_FF = 5632
FFN_RES_WEIGHT = 0.5
N_SUBLAYERS = 3
ROPE_BASE = 10000.0
EPS = 1e-6

LANES = 128
SUBLANES = 8
VMEM_LIMIT = 56 * 1024 * 1024

TM = 512
TM_FFN = 1024
TF = 512
TN_ADA = 1024
TQ = 512
HALO = 32
CONV_CB = 256
CONV_R = 64
ROW_CH = 16

F32 = jnp.float32
BF16 = jnp.bfloat16


def _rms(x, g):
    return x * lax.rsqrt(jnp.mean(x * x, axis=-1, keepdims=True) + EPS) * g


def _rep(row):
    return jnp.broadcast_to(row, (SUBLANES, row.shape[-1]))


def _unit_rms(x):
    r, d = x.shape
    x = x.reshape(r // SUBLANES, SUBLANES, d)
    return x * lax.rsqrt(jnp.mean(x * x, axis=-1, keepdims=True) + EPS)


def _prenorm_rows(g_ref, mod_ref, sub):
    shift = mod_ref[0, 3 * sub:3 * sub + 1, :]
    scale = mod_ref[0, 3 * sub + 1:3 * sub + 2, :]
    return _rep(g_ref[...] * (1.0 + scale)), _rep(shift)


def _dot(a, b):
    return jnp.dot(a, b, preferred_element_type=F32)


def _row_chunks(n_rows):
    return [slice(r, r + ROW_CH) for r in range(0, n_rows, ROW_CH)]


def _ordering_zero(v, shape, dtype):
    rows, lanes = v.shape
    t = jnp.sum(v.reshape(rows // SUBLANES, SUBLANES, lanes), axis=0)
    bits = lax.shift_right_logical(lax.shift_right_logical(pltpu.bitcast(t, jnp.int32), 16), 16)
    return jnp.tile(bits.astype(F32), (shape[0] // SUBLANES, shape[1] // lanes)).astype(dtype)


def _adaln_kernel(c_ref, w_ref, b_ref, o_ref):
    c = c_ref[...]
    sc = (c * jax.nn.sigmoid(c)).astype(BF16)
    o_ref[...] = _dot(sc, w_ref[...].astype(BF16)) + b_ref[...]


def _adaln(c, w, b):
    bsz, d = c.shape
    n = w.shape[1]
    return pl.pallas_call(
        _adaln_kernel,
        grid=(n // TN_ADA,),
        in_specs=[pl.BlockSpec((bsz, d), lambda j: (0, 0)),
                  pl.BlockSpec((d, TN_ADA), lambda j: (0, j)),
                  pl.BlockSpec((1, TN_ADA), lambda j: (0, j))],
        out_specs=pl.BlockSpec((bsz, TN_ADA), lambda j: (0, j)),
        out_shape=jax.ShapeDtypeStruct((bsz, n), F32),
        compiler_params=pltpu.CompilerParams(dimension_semantics=("arbitrary",),
                                             vmem_limit_bytes=VMEM_LIMIT),
        name="adaln_mod",
    )(c, w, b)


def _ffn_kernel(x_ref, mod_ref, gpre_ref, gpost_ref, wg_ref, wu_ref, wd_ref, o_ref, u_ref, *, sub):
    j = pl.program_id(1)

    @pl.when(j == 0)
    def _():
        gain, shift = _prenorm_rows(gpre_ref, mod_ref, sub)
        for rows in _row_chunks(TM_FFN):
            u = _unit_rms(x_ref[rows, :]) * gain + shift
            u_ref[rows, :] = u.reshape(ROW_CH, -1).astype(BF16)
        o_ref[...] = jnp.zeros_like(o_ref)

    u = u_ref[...]
    g = _dot(u, wg_ref[...])
    up = _dot(u, wu_ref[...])
    h = (g * jax.nn.sigmoid(g) * up).astype(BF16)
    o_ref[...] += _dot(h, wd_ref[...])

    @pl.when(j == pl.num_programs(1) - 1)
    def _():
        gate = _rep(FFN_RES_WEIGHT * mod_ref[0, 3 * sub + 2:3 * sub + 3, :] * gpost_ref[...])
        for rows in _row_chunks(TM_FFN):
            y = (_unit_rms(o_ref[rows, :]) * gate).reshape(ROW_CH, -1)
            o_ref[rows, :] = x_ref[rows, :] + y


def _ffn(x, mod, g_pre, g_post, wg, wu, wd, *, sub, seq):
    n, d = x.shape
    tiles_per_seq = seq // TM_FFN
    row = lambda i, j: (i, 0)
    const = lambda i, j: (0, 0)
    return pl.pallas_call(
        functools.partial(_ffn_kernel, sub=sub),
        grid=(n // TM_FFN, D_FF // TF),
        in_specs=[pl.BlockSpec((TM_FFN, d), row),
                  pl.BlockSpec((1, 3 * N_SUBLAYERS, d), lambda i, j: (i // tiles_per_seq, 0, 0)),
                  pl.BlockSpec((1, d), const),
                  pl.BlockSpec((1, d), const),
                  pl.BlockSpec((d, TF), lambda i, j: (0, j)),
                  pl.BlockSpec((d, TF), lambda i, j: (0, j)),
                  pl.BlockSpec((TF, d), lambda i, j: (j, 0))],
        out_specs=pl.BlockSpec((TM_FFN, d), row),
        out_shape=jax.ShapeDtypeStruct((n, d), F32),
        scratch_shapes=[pltpu.VMEM((TM_FFN, d), BF16)],
        compiler_params=pltpu.CompilerParams(dimension_semantics=("parallel", "arbitrary"),
                                             vmem_limit_bytes=VMEM_LIMIT),
        name=f"ffn{sub}",
    )(x, mod, g_pre, g_post, wg, wu, wd)


def _conv_kernel(x_ref, mod_ref, gpre_ref, w_ref, wdw_ref, bdw_ref, lng_ref, lnb_ref, gout_ref,
                 o_ref, u_ref, hp_ref, sh_ref, cv_ref):
    t = pl.program_id(1)

    @pl.when(t == 0)
    def _():
        hp_ref[0:HALO, :] = jnp.zeros((HALO, D_CONV), F32)

    gain, shift = _prenorm_rows(gpre_ref, mod_ref, 1)
    for rows in _row_chunks(TM):
        u = _unit_rms(x_ref[rows, :]) * gain + shift
        u_ref[0, rows, :] = u.reshape(ROW_CH, -1).astype(BF16)
    u = u_ref[0]

    first = HALO - (CONV_WIDTH - 1)
    sh_rows = TM + HALO - SUBLANES
    for cb in range(D_CONV // CONV_CB):
        cols = slice(cb * CONV_CB, (cb + 1) * CONV_CB)
        gcols = slice(D_CONV + cb * CONV_CB, D_CONV + (cb + 1) * CONV_CB)
        val = _dot(u, w_ref[:, cols])
        hp_ref[HALO:HALO + TM, cols] = val * jax.nn.sigmoid(_dot(u, w_ref[:, gcols]))
        for s in range(1, SUBLANES):
            sh_ref[s - 1, :, cols] = hp_ref[s:s + sh_rows, cols]
        for r0 in range(0, TM, CONV_R):
            acc = jnp.zeros((CONV_R // SUBLANES, SUBLANES, CONV_CB), F32)
            for k in range(CONV_WIDTH):
                s, q = (first + k) % SUBLANES, (first + k) // SUBLANES
                start = r0 + q * SUBLANES
                if s == 0:
                    win = hp_ref[start:start + CONV_R, cols]
                else:
                    win = sh_ref[s - 1, start:start + CONV_R, cols]
                win = win.reshape(CONV_R // SUBLANES, SUBLANES, CONV_CB)
                acc = acc + win * wdw_ref[k, :, cols][None]
            cv_ref[r0:r0 + CONV_R, cols] = acc.reshape(CONV_R, CONV_CB)

    hp_ref[0:HALO, :] = hp_ref[TM:TM + HALO, :]

    bias, ln_g, ln_b, g_out = _rep(bdw_ref[...]), _rep(lng_ref[...]), _rep(lnb_ref[...]), _rep(gout_ref[...])
    for rows in _row_chunks(TM):
        cv = cv_ref[rows, :].reshape(ROW_CH // SUBLANES, SUBLANES, D_CONV) + bias
        xc = cv - jnp.mean(cv, axis=-1, keepdims=True)
        var = jnp.mean(xc * xc, axis=-1, keepdims=True)
        y = xc * lax.rsqrt(var + EPS) * ln_g + ln_b
        y = y * jax.nn.sigmoid(y)
        y = y * lax.rsqrt(jnp.mean(y * y, axis=-1, keepdims=True) + EPS) * g_out
        o_ref[0, rows, :] = y.reshape(ROW_CH, D_CONV).astype(BF16)


def _conv_branch(x, mod, g_pre, w_conv, wdw_b, b_dw, ln_g, ln_b, g_out):
    bsz, seq, d = x.shape
    const2 = lambda b, t: (0, 0)
    return pl.pallas_call(
        _conv_kernel,
        grid=(bsz, seq // TM),
        in_specs=[pl.BlockSpec((None, TM, d), lambda b, t: (b, t, 0)),
                  pl.BlockSpec((1, 3 * N_SUBLAYERS, d), lambda b, t: (b, 0, 0)),
                  pl.BlockSpec((1, d), const2),
                  pl.BlockSpec((d, 2 * D_CONV), const2),
                  pl.BlockSpec((CONV_WIDTH, SUBLANES, D_CONV), lambda b, t: (0, 0, 0)),
                  pl.BlockSpec((1, D_CONV), const2),
                  pl.BlockSpec((1, D_CONV), const2),
                  pl.BlockSpec((1, D_CONV), const2),
                  pl.BlockSpec((1, D_CONV), const2)],
        out_specs=[pl.BlockSpec((1, TM, D_CONV), lambda b, t: (b, t, 0)),
                   pl.BlockSpec((1, TM, d), lambda b, t: (b, t, 0))],
        out_shape=[jax.ShapeDtypeStruct((bsz, seq, D_CONV), BF16),
                   jax.ShapeDtypeStruct((bsz, seq, d), BF16)],
        scratch_shapes=[pltpu.VMEM((TM + HALO, D_CONV), F32),
                        pltpu.VMEM((SUBLANES - 1, TM + HALO - SUBLANES, D_CONV), F32),
                        pltpu.VMEM((TM, D_CONV), F32)],
        compiler_params=pltpu.CompilerParams(dimension_semantics=("parallel", "arbitrary"),
                                             vmem_limit_bytes=VMEM_LIMIT),
        name="conv_branch",
    )(x, mod, g_pre, w_conv, wdw_b, b_dw, ln_g, ln_b, g_out)


def _rope(xr, cos, sin_lo, sin_hi):
    return (xr * cos + pltpu.roll(xr, LANES - QK_ROPE_DIM // 2, axis=1) * sin_lo
            + pltpu.roll(xr, QK_ROPE_DIM // 2, axis=1) * sin_hi)


def _mla_proj_kernel(u_ref, pos_ref, invf_ref, wq_ref, wkv_ref, wkr_ref,
                     gq_ref, gkv_ref, wqn_ref, wqr_ref, wuk_ref, wuv_ref,
                     qn_ref, qr_ref, kn_ref, kr_ref, v_ref):
    u = u_ref[...]

    ang = pos_ref[...].astype(F32) * invf_ref[...]
    cos = jnp.cos(ang)
    sin = jnp.sin(ang)
    lane = lax.broadcasted_iota(jnp.int32, ang.shape, 1)
    half = QK_ROPE_DIM // 2
    sin_lo = jnp.where(lane < half, -sin, 0.0)
    sin_hi = jnp.where((lane >= half) & (lane < QK_ROPE_DIM), sin, 0.0)

    scale = (QK_NOPE_DIM + QK_ROPE_DIM) ** -0.5
    kvlat = _dot(u, wkv_ref[...])
    qlat = _dot(u, wq_ref[...])
    kr = _dot(u, wkr_ref[...] + _ordering_zero(cos + sin, wkr_ref.shape, BF16))
    ckv = _rms(kvlat, gkv_ref[...]).astype(BF16)
    kn_ref[...] = _dot(ckv, wuk_ref[...]).astype(BF16)
    v_ref[...] = _dot(ckv, wuv_ref[...]).astype(BF16)
    qlat = _rms(qlat, gq_ref[...]).astype(BF16)
    qn_ref[...] = (_dot(qlat, wqn_ref[...]) * scale).astype(BF16)
    qr = _dot(qlat, wqr_ref[...])
    kr_ref[...] = _rope(kr, cos, sin_lo, sin_hi).astype(BF16)
    for h in range(N_HEADS):
        cols = slice(h * LANES, (h + 1) * LANES)
        qr_ref[:, cols] = (_rope(qr[:, cols], cos, sin_lo, sin_hi) * scale).astype(BF16)


def _mla_proj(u, pos, invf, wq, wkv, wkr, gq, gkv, wqn, wqr, wuk, wuv):
    n, d = u.shape
    row = lambda i: (i, 0)
    const = lambda i: (0, 0)
    full = lambda a: pl.BlockSpec(a.shape, const)
    wide = jax.ShapeDtypeStruct((n, N_HEADS * LANES), BF16)
    return pl.pallas_call(
        _mla_proj_kernel,
        grid=(n // TM,),
        in_specs=[pl.BlockSpec((TM, d), row),
                  pl.BlockSpec((TM, 1), row),
                  full(invf), full(wq), full(wkv), full(wkr), full(gq), full(gkv),
                  full(wqn), full(wqr), full(wuk), full(wuv)],
        out_specs=[pl.BlockSpec((TM, N_HEADS * LANES), row),
                   pl.BlockSpec((TM, N_HEADS * LANES), row),
                   pl.BlockSpec((TM, N_HEADS * LANES), row),
                   pl.BlockSpec((TM, LANES), row),
                   pl.BlockSpec((TM, N_HEADS * LANES), row)],
        out_shape=[wide, wide, wide, jax.ShapeDtypeStruct((n, LANES), BF16), wide],
        compiler_params=pltpu.CompilerParams(dimension_semantics=("parallel",),
                                             vmem_limit_bytes=VMEM_LIMIT),
        name="mla_proj",
    )(u, pos, invf, wq, wkv, wkr, gq, gkv, wqn, wqr, wuk, wuv)


def _dot_nt(a, b):
    return lax.dot_general(a, b, (((1,), (1,)), ((), ())), preferred_element_type=F32)


def _attn_kernel(*refs, n_cast):
    qn_ref, qr_ref, kn_ref, kr_ref, v_ref = refs[:5]
    o_ref = refs[5 + n_cast]
    for src, dst in zip(refs[5:5 + n_cast], refs[6 + n_cast:]):
        dst[...] = src[...].astype(BF16)
    seq = o_ref.shape[1]
    ki = lax.broadcasted_iota(jnp.int32, (TQ, TQ), 0)
    qj = lax.broadcasted_iota(jnp.int32, (TQ, TQ), 1)
    v_t = v_ref[0].astype(F32).T.astype(BF16)
    for qi in range(seq // TQ):
        diag = slice(qi * TQ, (qi + 1) * TQ)
        past = slice(0, qi * TQ)
        q = jnp.concatenate([qn_ref[0, diag, :], qr_ref[0, diag, :]], axis=-1)
        k_d = jnp.concatenate([kn_ref[0, diag, :], kr_ref[0, diag, :]], axis=-1)
        s_d = jnp.where(ki <= qj, _dot_nt(k_d, q), -jnp.inf)
        m = jnp.max(s_d, axis=0, keepdims=True)
        if qi:
            k_p = jnp.concatenate([kn_ref[0, past, :], kr_ref[0, past, :]], axis=-1)
            s_p = _dot_nt(k_p, q)
            m = jnp.maximum(m, jnp.max(s_p, axis=0, keepdims=True))
        p_d = jnp.exp(s_d - m)
        l = jnp.sum(p_d, axis=0, keepdims=True)
        acc = _dot(v_t[:, diag], p_d.astype(BF16))
        if qi:
            p_p = jnp.exp(s_p - m)
            l = l + jnp.sum(p_p, axis=0, keepdims=True)
            acc = acc + _dot(v_t[:, past], p_p.astype(BF16))
        o_ref[0, diag, :] = (acc * (1.0 / l)).T.astype(BF16)


def _cast_row_spec(a, n_steps):
    rows, repeat = a.shape[0], 1
    while rows % (n_steps // repeat) or (rows // (n_steps // repeat)) % (2 * SUBLANES):
        repeat *= 2
    return pl.BlockSpec((rows // (n_steps // repeat), a.shape[1]),
                        lambda b, h: ((b * N_HEADS + h) // repeat, 0))


def _attention(qn, qr, kn, kr, v, cast=()):
    bsz, seq, _ = qn.shape
    head = pl.BlockSpec((1, seq, LANES), lambda b, h: (b, 0, h))
    cast_specs = [_cast_row_spec(a, bsz * N_HEADS) for a in cast]
    return pl.pallas_call(
        functools.partial(_attn_kernel, n_cast=len(cast)),
        grid=(bsz, N_HEADS),
        in_specs=[head, head, head, pl.BlockSpec((1, seq, LANES), lambda b, h: (b, 0, 0)), head]
                 + cast_specs,
        out_specs=[head] + cast_specs,
        out_shape=[jax.ShapeDtypeStruct((bsz, seq, D_ATTN), BF16)]
                  + [jax.ShapeDtypeStruct(a.shape, BF16) for a in cast],
        compiler_params=pltpu.CompilerParams(dimension_semantics=("arbitrary", "arbitrary"),
                                             vmem_limit_bytes=VMEM_LIMIT),
        name="mla_attn",
    )(qn, qr, kn, kr, v, *cast)


def _mix_out_kernel(x_ref, mod_ref, cv_ref, at_ref, gattn_ref, wc_ref, wa_ref, gpost_ref, o_ref,
                    an_ref, y_ref):
    g_attn = _rep(gattn_ref[...])
    for rows in _row_chunks(TM):
        a = _unit_rms(at_ref[rows, :].astype(F32)) * g_attn
        an_ref[rows, :] = a.reshape(ROW_CH, -1).astype(BF16)
    y_ref[...] = _dot(cv_ref[...], wc_ref[...]) + _dot(an_ref[...], wa_ref[...])
    gate = _rep(mod_ref[0, 5:6, :] * gpost_ref[...])
    for rows in _row_chunks(TM):
        y = (_unit_rms(y_ref[rows, :]) * gate).reshape(ROW_CH, -1)
        o_ref[rows, :] = x_ref[rows, :] + y


def _mix_out(x, mod, cv, at, g_attn, wc, wa, g_post, *, seq):
    n, d = x.shape
    tiles_per_seq = seq // TM
    row = lambda i: (i, 0)
    const = lambda i: (0, 0)
    return pl.pallas_call(
        _mix_out_kernel,
        grid=(n // TM,),
        in_specs=[pl.BlockSpec((TM, d), row),
                  pl.BlockSpec((1, 3 * N_SUBLAYERS, d), lambda i: (i // tiles_per_seq, 0, 0)),
                  pl.BlockSpec((TM, D_CONV), row),
                  pl.BlockSpec((TM, D_ATTN), row),
                  pl.BlockSpec((1, D_ATTN), const),
                  pl.BlockSpec((D_CONV, d), const),
                  pl.BlockSpec((D_ATTN, d), const),
                  pl.BlockSpec((1, d), const)],
        out_specs=pl.BlockSpec((TM, d), row),
        out_shape=jax.ShapeDtypeStruct((n, d), F32),
        scratch_shapes=[pltpu.VMEM((TM, D_ATTN), BF16), pltpu.VMEM((TM, d), F32)],
        compiler_params=pltpu.CompilerParams(dimension_semantics=("parallel",),
                                             vmem_limit_bytes=VMEM_LIMIT),
        name="mix_out",
    )(x, mod, cv, at, g_attn, wc, wa, g_post)


def _rope_columns(w, n_groups, group, lo):
    k = w.shape[0]
    w = w.reshape(k, n_groups, group)[:, :, lo:lo + QK_ROPE_DIM]
    w = jnp.pad(w, ((0, 0), (0, 0), (0, LANES - QK_ROPE_DIM)))
    return w.reshape(k, n_groups * LANES)


def kernel(x, c, positions, w_ada, b_ada, g_pre_ffn1, w1_gate, w1_up, w1_down, g_post_ffn1, g_pre_mix, w_in, w_dw, b_dw, ln_conv_g, ln_conv_b, g_q_lat, w_uq, g_kv_lat, w_uk, w_uv, g_conv_out, g_attn_out, w_out, g_post_mix, g_pre_ffn2, w2_gate, w2_up, w2_down, g_post_ffn2):
    bsz, seq, d = x.shape
    n = bsz * seq
    depth = w_ada.shape[0]
    half = QK_ROPE_DIM // 2
    inv_freq = ROPE_BASE ** (-jnp.arange(half, dtype=F32) / half)
    invf = jnp.concatenate([inv_freq, inv_freq, jnp.zeros((LANES - QK_ROPE_DIM,), F32)])[None, :]
    pos = positions.reshape(n, 1)
    bf = lambda a: a.astype(BF16)

    xf = x.reshape(n, d)
    for l in range(depth):
        mod = _adaln(c, w_ada[l], b_ada[l][None, :]).reshape(bsz, 3 * N_SUBLAYERS, d)

        xf = _ffn(xf, mod, g_pre_ffn1[l][None], g_post_ffn1[l][None],
                  bf(w1_gate[l]), bf(w1_up[l]), bf(w1_down[l]), sub=0, seq=seq)

        i1 = 2 * D_CONV
        i2 = i1 + Q_LORA_RANK
        i3 = i2 + KV_LORA_RANK
        wi = w_in[l]
        cv, u = _conv_branch(xf.reshape(bsz, seq, d), mod, g_pre_mix[l][None], bf(wi[:, :i1]),
                             jnp.broadcast_to(w_dw[l][:, None, :], (CONV_WIDTH, SUBLANES, D_CONV)),
                             b_dw[l][None], ln_conv_g[l][None], ln_conv_b[l][None], g_conv_out[l][None])

        qk = QK_NOPE_DIM + QK_ROPE_DIM
        wqn = w_uq[l].reshape(Q_LORA_RANK, N_HEADS, qk)[:, :, :QK_NOPE_DIM].reshape(Q_LORA_RANK, -1)
        wqr = _rope_columns(w_uq[l], N_HEADS, qk, QK_NOPE_DIM)
        wkr = _rope_columns(wi[:, i3:], 1, QK_ROPE_DIM, 0)
        qn, qr, kn, kr, v = _mla_proj(
            u.reshape(n, d), pos, invf, bf(wi[:, i1:i2]), bf(wi[:, i2:i3]), bf(wkr),
            g_q_lat[l][None], g_kv_lat[l][None], bf(wqn), bf(wqr), bf(w_uk[l]), bf(w_uv[l]))

        shp = lambda a: a.reshape(bsz, seq, a.shape[-1])
        at, w2g, w2u, w2d = _attention(shp(qn), shp(qr), shp(kn), shp(kr), shp(v),
                                       cast=(w2_gate[l], w2_up[l], w2_down[l]))

        wo = bf(w_out[l])
        xf = _mix_out(xf, mod, cv.reshape(n, D_CONV), at.reshape(n, D_ATTN), g_attn_out[l][None],
                      wo[:D_CONV], wo[D_CONV:], g_post_mix[l][None], seq=seq)

        xf = _ffn(xf, mod, g_pre_ffn2[l][None], g_post_ffn2[l][None], w2g, w2u, w2d, sub=2, seq=seq)
    return xf.reshape(bsz, seq, d)
```

```python
import functools

import jax
import jax.numpy as jnp
from jax import lax
from jax.experimental import pallas as pl
from jax.experimental.pallas import tpu as pltpu

D_MODEL = 2048
D_CONV = 1024
CONV_WIDTH = 31
N_HEADS = 8
QK_NOPE_DIM = 128
QK_ROPE_DIM = 64
V_HEAD_DIM = 128
Q_LORA_RANK = 768
KV_LORA_RANK = 512
D_ATTN = N_HEADS * V_HEAD_DIM
D_FF = 5632
FFN_RES_WEIGHT = 0.5
N_SUBLAYERS = 3
ROPE_BASE = 10000.0
EPS = 1e-6

LANES = 128
SUBLANES = 8
VMEM_LIMIT = 56 * 1024 * 1024

TM = 512
TM_FFN = 1024
TF = 512
TN_ADA = 1024
TQ = 512
HALO = 32
CONV_CB = 256
CONV_R = 64
ROW_CH = 16

F32 = jnp.float32
BF16 = jnp.bfloat16


def _rms(x, g):
    return x * lax.rsqrt(jnp.mean(x * x, axis=-1, keepdims=True) + EPS) * g


def _rep(row):
    return jnp.broadcast_to(row, (SUBLANES, row.shape[-1]))


def _unit_rms(x):
    r, d = x.shape
    x = x.reshape(r // SUBLANES, SUBLANES, d)
    return x * lax.rsqrt(jnp.mean(x * x, axis=-1, keepdims=True) + EPS)


def _prenorm_rows(g_ref, mod_ref, sub):
    shift = mod_ref[0, 3 * sub:3 * sub + 1, :]
    scale = mod_ref[0, 3 * sub + 1:3 * sub + 2, :]
    return _rep(g_ref[...] * (1.0 + scale)), _rep(shift)


def _dot(a, b):
    return jnp.dot(a, b, preferred_element_type=F32)


def _row_chunks(n_rows):
    return [slice(r, r + ROW_CH) for r in range(0, n_rows, ROW_CH)]


def _ordering_zero(v, shape, dtype):
    rows, lanes = v.shape
    t = jnp.sum(v.reshape(rows // SUBLANES, SUBLANES, lanes), axis=0)
    bits = lax.shift_right_logical(lax.shift_right_logical(pltpu.bitcast(t, jnp.int32), 16), 16)
    return jnp.tile(bits.astype(F32), (shape[0] // SUBLANES, shape[1] // lanes)).astype(dtype)


def _adaln_kernel(c_ref, w_ref, b_ref, o_ref):
    c = c_ref[...]
    sc = (c * jax.nn.sigmoid(c)).astype(BF16)
    o_ref[...] = _dot(sc, w_ref[...].astype(BF16)) + b_ref[...]


def _adaln(c, w, b):
    bsz, d = c.shape
    n = w.shape[1]
    return pl.pallas_call(
        _adaln_kernel,
        grid=(n // TN_ADA,),
        in_specs=[pl.BlockSpec((bsz, d), lambda j: (0, 0)),
                  pl.BlockSpec((d, TN_ADA), lambda j: (0, j)),
                  pl.BlockSpec((1, TN_ADA), lambda j: (0, j))],
        out_specs=pl.BlockSpec((bsz, TN_ADA), lambda j: (0, j)),
        out_shape=jax.ShapeDtypeStruct((bsz, n), F32),
        compiler_params=pltpu.CompilerParams(dimension_semantics=("arbitrary",),
                                             vmem_limit_bytes=VMEM_LIMIT),
        name="adaln_mod",
    )(c, w, b)


def _ffn_kernel(x_ref, mod_ref, gpre_ref, gpost_ref, wg_ref, wu_ref, wd_ref, o_ref, u_ref, *, sub):
    j = pl.program_id(1)

    def swiglu_block():
        u = u_ref[...]
        g = _dot(u, wg_ref[...])
        up = _dot(u, wu_ref[...])
        h = (g * jax.nn.sigmoid(g) * up).astype(BF16)
        return _dot(h, wd_ref[...])

    @pl.when(j == 0)
    def _():
        gain, shift = _prenorm_rows(gpre_ref, mod_ref, sub)
        for rows in _row_chunks(TM_FFN):
            u = _unit_rms(x_ref[rows, :]) * gain + shift
            u_ref[rows, :] = u.reshape(ROW_CH, -1).astype(BF16)
        o_ref[...] = swiglu_block()

    @pl.when(j > 0)
    def _():
        o_ref[...] += swiglu_block()

    @pl.when(j == pl.num_programs(1) - 1)
    def _():
        gate = _rep(FFN_RES_WEIGHT * mod_ref[0, 3 * sub + 2:3 * sub + 3, :] * gpost_ref[...])
        for rows in _row_chunks(TM_FFN):
            y = (_unit_rms(o_ref[rows, :]) * gate).reshape(ROW_CH, -1)
            o_ref[rows, :] = x_ref[rows, :] + y


def _ffn(x, mod, g_pre, g_post, wg, wu, wd, *, sub, seq):
    n, d = x.shape
    tiles_per_seq = seq // TM_FFN
    row = lambda i, j: (i, 0)
    const = lambda i, j: (0, 0)
    return pl.pallas_call(
        functools.partial(_ffn_kernel, sub=sub),
        grid=(n // TM_FFN, D_FF // TF),
        in_specs=[pl.BlockSpec((TM_FFN, d), row),
                  pl.BlockSpec((1, 3 * N_SUBLAYERS, d), lambda i, j: (i // tiles_per_seq, 0, 0)),
                  pl.BlockSpec((1, d), const),
                  pl.BlockSpec((1, d), const),
                  pl.BlockSpec((d, TF), lambda i, j: (0, j)),
                  pl.BlockSpec((d, TF), lambda i, j: (0, j)),
                  pl.BlockSpec((TF, d), lambda i, j: (j, 0))],
        out_specs=pl.BlockSpec((TM_FFN, d), row),
        out_shape=jax.ShapeDtypeStruct((n, d), F32),
        scratch_shapes=[pltpu.VMEM((TM_FFN, d), BF16)],
        compiler_params=pltpu.CompilerParams(dimension_semantics=("parallel", "arbitrary"),
                                             vmem_limit_bytes=VMEM_LIMIT),
        name=f"ffn{sub}",
    )(x, mod, g_pre, g_post, wg, wu, wd)


def _conv_kernel(x_ref, mod_ref, gpre_ref, w_ref, wdw_ref, bdw_ref, lng_ref, lnb_ref, gout_ref,
                 o_ref, u_ref, hp_ref, sh_ref, cv_ref):
    t = pl.program_id(1)

    @pl.when(t == 0)
    def _():
        hp_ref[0:HALO, :] = jnp.zeros((HALO, D_CONV), F32)

    gain, shift = _prenorm_rows(gpre_ref, mod_ref, 1)
    for rows in _row_chunks(TM):
        u = _unit_rms(x_ref[rows, :]) * gain + shift
        u_ref[0, rows, :] = u.reshape(ROW_CH, -1).astype(BF16)
    u = u_ref[0]

    first = HALO - (CONV_WIDTH - 1)
    sh_rows = TM + HALO - SUBLANES
    for cb in range(D_CONV // CONV_CB):
        cols = slice(cb * CONV_CB, (cb + 1) * CONV_CB)
        gcols = slice(D_CONV + cb * CONV_CB, D_CONV + (cb + 1) * CONV_CB)
        val = _dot(u, w_ref[:, cols])
        hp_ref[HALO:HALO + TM, cols] = val * jax.nn.sigmoid(_dot(u, w_ref[:, gcols]))
        for s in range(1, SUBLANES):
            sh_ref[s - 1, :, cols] = hp_ref[s:s + sh_rows, cols]
        for r0 in range(0, TM, CONV_R):
            acc = jnp.zeros((CONV_R // SUBLANES, SUBLANES, CONV_CB), F32)
            for k in range(CONV_WIDTH):
                s, q = (first + k) % SUBLANES, (first + k) // SUBLANES
                start = r0 + q * SUBLANES
                if s == 0:
                    win = hp_ref[start:start + CONV_R, cols]
                else:
                    win = sh_ref[s - 1, start:start + CONV_R, cols]
                win = win.reshape(CONV_R // SUBLANES, SUBLANES, CONV_CB)
                acc = acc + win * wdw_ref[k, :, cols][None]
            cv_ref[r0:r0 + CONV_R, cols] = acc.reshape(CONV_R, CONV_CB)

    hp_ref[0:HALO, :] = hp_ref[TM:TM + HALO, :]

    bias, ln_g, ln_b, g_out = _rep(bdw_ref[...]), _rep(lng_ref[...]), _rep(lnb_ref[...]), _rep(gout_ref[...])
    for rows in _row_chunks(TM):
        cv = cv_ref[rows, :].reshape(ROW_CH // SUBLANES, SUBLANES, D_CONV) + bias
        xc = cv - jnp.mean(cv, axis=-1, keepdims=True)
        var = jnp.mean(xc * xc, axis=-1, keepdims=True)
        y = xc * lax.rsqrt(var + EPS) * ln_g + ln_b
        y = y * jax.nn.sigmoid(y)
        y = y * lax.rsqrt(jnp.mean(y * y, axis=-1, keepdims=True) + EPS) * g_out
        o_ref[0, rows, :] = y.reshape(ROW_CH, D_CONV).astype(BF16)


def _conv_branch(x, mod, g_pre, w_conv, wdw_b, b_dw, ln_g, ln_b, g_out):
    bsz, seq, d = x.shape
    const2 = lambda b, t: (0, 0)
    return pl.pallas_call(
        _conv_kernel,
        grid=(bsz, seq // TM),
        in_specs=[pl.BlockSpec((None, TM, d), lambda b, t: (b, t, 0)),
                  pl.BlockSpec((1, 3 * N_SUBLAYERS, d), lambda b, t: (b, 0, 0)),
                  pl.BlockSpec((1, d), const2),
                  pl.BlockSpec((d, 2 * D_CONV), const2),
                  pl.BlockSpec((CONV_WIDTH, SUBLANES, D_CONV), lambda b, t: (0, 0, 0)),
                  pl.BlockSpec((1, D_CONV), const2),
                  pl.BlockSpec((1, D_CONV), const2),
                  pl.BlockSpec((1, D_CONV), const2),
                  pl.BlockSpec((1, D_CONV), const2)],
        out_specs=[pl.BlockSpec((1, TM, D_CONV), lambda b, t: (b, t, 0)),
                   pl.BlockSpec((1, TM, d), lambda b, t: (b, t, 0))],
        out_shape=[jax.ShapeDtypeStruct((bsz, seq, D_CONV), BF16),
                   jax.ShapeDtypeStruct((bsz, seq, d), BF16)],
        scratch_shapes=[pltpu.VMEM((TM + HALO, D_CONV), F32),
                        pltpu.VMEM((SUBLANES - 1, TM + HALO - SUBLANES, D_CONV), F32),
                        pltpu.VMEM((TM, D_CONV), F32)],
        compiler_params=pltpu.CompilerParams(dimension_semantics=("parallel", "arbitrary"),
                                             vmem_limit_bytes=VMEM_LIMIT),
        name="conv_branch",
    )(x, mod, g_pre, w_conv, wdw_b, b_dw, ln_g, ln_b, g_out)


def _rope(xr, cos, sin_lo, sin_hi):
    return (xr * cos + pltpu.roll(xr, LANES - QK_ROPE_DIM // 2, axis=1) * sin_lo
            + pltpu.roll(xr, QK_ROPE_DIM // 2, axis=1) * sin_hi)


def _mla_proj_kernel(u_ref, pos_ref, invf_ref, wq_ref, wkv_ref, wkr_ref,
                     gq_ref, gkv_ref, wqn_ref, wqr_ref, wuk_ref, wuv_ref,
                     qn_ref, qr_ref, kn_ref, kr_ref, v_ref):
    u = u_ref[...]

    ang = pos_ref[...].astype(F32) * invf_ref[...]
    cos = jnp.cos(ang)
    sin = jnp.sin(ang)
    lane = lax.broadcasted_iota(jnp.int32, ang.shape, 1)
    half = QK_ROPE_DIM // 2
    sin_lo = jnp.where(lane < half, -sin, 0.0)
    sin_hi = jnp.where((lane >= half) & (lane < QK_ROPE_DIM), sin, 0.0)

    scale = (QK_NOPE_DIM + QK_ROPE_DIM) ** -0.5
    kvlat = _dot(u, wkv_ref[...])
    qlat = _dot(u, wq_ref[...])
    kr = _dot(u, wkr_ref[...] + _ordering_zero(cos + sin, wkr_ref.shape, BF16))
    ckv = _rms(kvlat, gkv_ref[...]).astype(BF16)
    kn_ref[...] = _dot(ckv, wuk_ref[...]).astype(BF16)
    v_ref[...] = _dot(ckv, wuv_ref[...]).astype(BF16)
    qlat = _rms(qlat, gq_ref[...]).astype(BF16)
    qn_ref[...] = (_dot(qlat, wqn_ref[...]) * scale).astype(BF16)
    qr = _dot(qlat, wqr_ref[...])
    kr_ref[...] = _rope(kr, cos, sin_lo, sin_hi).astype(BF16)
    for h in range(N_HEADS):
        cols = slice(h * LANES, (h + 1) * LANES)
        qr_ref[:, cols] = (_rope(qr[:, cols], cos, sin_lo, sin_hi) * scale).astype(BF16)


def _mla_proj(u, pos, invf, wq, wkv, wkr, gq, gkv, wqn, wqr, wuk, wuv):
    n, d = u.shape
    row = lambda i: (i, 0)
    const = lambda i: (0, 0)
    full = lambda a: pl.BlockSpec(a.shape, const)
    wide = jax.ShapeDtypeStruct((n, N_HEADS * LANES), BF16)
    return pl.pallas_call(
        _mla_proj_kernel,
        grid=(n // TM,),
        in_specs=[pl.BlockSpec((TM, d), row),
                  pl.BlockSpec((TM, 1), row),
                  full(invf), full(wq), full(wkv), full(wkr), full(gq), full(gkv),
                  full(wqn), full(wqr), full(wuk), full(wuv)],
        out_specs=[pl.BlockSpec((TM, N_HEADS * LANES), row),
                   pl.BlockSpec((TM, N_HEADS * LANES), row),
                   pl.BlockSpec((TM, N_HEADS * LANES), row),
                   pl.BlockSpec((TM, LANES), row),
                   pl.BlockSpec((TM, N_HEADS * LANES), row)],
        out_shape=[wide, wide, wide, jax.ShapeDtypeStruct((n, LANES), BF16), wide],
        compiler_params=pltpu.CompilerParams(dimension_semantics=("parallel",),
                                             vmem_limit_bytes=VMEM_LIMIT),
        name="mla_proj",
    )(u, pos, invf, wq, wkv, wkr, gq, gkv, wqn, wqr, wuk, wuv)


def _dot_nt(a, b):
    return lax.dot_general(a, b, (((1,), (1,)), ((), ())), preferred_element_type=F32)


def _attn_kernel(*refs, n_cast):
    qn_ref, qr_ref, kn_ref, kr_ref, v_ref = refs[:5]
    o_ref = refs[5 + n_cast]
    for src, dst in zip(refs[5:5 + n_cast], refs[6 + n_cast:]):
        dst[...] = src[...].astype(BF16)
    seq = o_ref.shape[1]
    ki = lax.broadcasted_iota(jnp.int32, (TQ, TQ), 0)
    qj = lax.broadcasted_iota(jnp.int32, (TQ, TQ), 1)
    v_t = v_ref[0].astype(F32).T.astype(BF16)
    for qi in range(seq // TQ):
        diag = slice(qi * TQ, (qi + 1) * TQ)
        past = slice(0, qi * TQ)
        q = jnp.concatenate([qn_ref[0, diag, :], qr_ref[0, diag, :]], axis=-1)
        k_d = jnp.concatenate([kn_ref[0, diag, :], kr_ref[0, diag, :]], axis=-1)
        s_d = jnp.where(ki <= qj, _dot_nt(k_d, q), -jnp.inf)
        m = jnp.max(s_d, axis=0, keepdims=True)
        if qi:
            k_p = jnp.concatenate([kn_ref[0, past, :], kr_ref[0, past, :]], axis=-1)
            s_p = _dot_nt(k_p, q)
            m = jnp.maximum(m, jnp.max(s_p, axis=0, keepdims=True))
        p_d = jnp.exp(s_d - m)
        l = jnp.sum(p_d, axis=0, keepdims=True)
        acc = _dot(v_t[:, diag], p_d.astype(BF16))
        if qi:
            p_p = jnp.exp(s_p - m)
            l = l + jnp.sum(p_p, axis=0, keepdims=True)
            acc = acc + _dot(v_t[:, past], p_p.astype(BF16))
        o_ref[0, diag, :] = (acc * (1.0 / l)).T.astype(BF16)


def _cast_row_spec(a, n_steps):
    rows, repeat = a.shape[0], 1
    while rows % (n_steps // repeat) or (rows // (n_steps // repeat)) % (2 * SUBLANES):
        repeat *= 2
    return pl.BlockSpec((rows // (n_steps // repeat), a.shape[1]),
                        lambda b, h: ((b * N_HEADS + h) // repeat, 0))


def _attention(qn, qr, kn, kr, v, cast=()):
    bsz, seq, _ = qn.shape
    head = pl.BlockSpec((1, seq, LANES), lambda b, h: (b, 0, h))
    cast_specs = [_cast_row_spec(a, bsz * N_HEADS) for a in cast]
    return pl.pallas_call(
        functools.partial(_attn_kernel, n_cast=len(cast)),
        grid=(bsz, N_HEADS),
        in_specs=[head, head, head, pl.BlockSpec((1, seq, LANES), lambda b, h: (b, 0, 0)), head]
                 + cast_specs,
        out_specs=[head] + cast_specs,
        out_shape=[jax.ShapeDtypeStruct((bsz, seq, D_ATTN), BF16)]
                  + [jax.ShapeDtypeStruct(a.shape, BF16) for a in cast],
        compiler_params=pltpu.CompilerParams(dimension_semantics=("arbitrary", "arbitrary"),
                                             vmem_limit_bytes=VMEM_LIMIT),
        name="mla_attn",
    )(qn, qr, kn, kr, v, *cast)


def _mix_out_kernel(x_ref, mod_ref, cv_ref, at_ref, gattn_ref, wc_ref, wa_ref, gpost_ref, o_ref,
                    an_ref, y_ref):
    g_attn = _rep(gattn_ref[...])
    for rows in _row_chunks(TM):
        a = _unit_rms(at_ref[rows, :].astype(F32)) * g_attn
        an_ref[rows, :] = a.reshape(ROW_CH, -1).astype(BF16)
    y_ref[...] = _dot(cv_ref[...], wc_ref[...]) + _dot(an_ref[...], wa_ref[...])
    gate = _rep(mod_ref[0, 5:6, :] * gpost_ref[...])
    for rows in _row_chunks(TM):
        y = (_unit_rms(y_ref[rows, :]) * gate).reshape(ROW_CH, -1)
        o_ref[rows, :] = x_ref[rows, :] + y


def _mix_out(x, mod, cv, at, g_attn, w_out, g_post, *, seq):
    n, d = x.shape
    tiles_per_seq = seq // TM
    row = lambda i: (i, 0)
    const = lambda i: (0, 0)
    return pl.pallas_call(
        _mix_out_kernel,
        grid=(n // TM,),
        in_specs=[pl.BlockSpec((TM, d), row),
                  pl.BlockSpec((1, 3 * N_SUBLAYERS, d), lambda i: (i // tiles_per_seq, 0, 0)),
                  pl.BlockSpec((TM, D_CONV), row),
                  pl.BlockSpec((TM, D_ATTN), row),
                  pl.BlockSpec((1, D_ATTN), const),
                  pl.BlockSpec((D_CONV, d), const),
                  pl.BlockSpec((D_ATTN, d), lambda i: (D_CONV // D_ATTN, 0)),
                  pl.BlockSpec((1, d), const)],
        out_specs=pl.BlockSpec((TM, d), row),
        out_shape=jax.ShapeDtypeStruct((n, d), F32),
        scratch_shapes=[pltpu.VMEM((TM, D_ATTN), BF16), pltpu.VMEM((TM, d), F32)],
        compiler_params=pltpu.CompilerParams(dimension_semantics=("parallel",),
                                             vmem_limit_bytes=VMEM_LIMIT),
        name="mix_out",
    )(x, mod, cv, at, g_attn, w_out, w_out, g_post)


def _rope_columns(w, n_groups, group, lo):
    k = w.shape[0]
    w = w.reshape(k, n_groups, group)[:, :, lo:lo + QK_ROPE_DIM]
    w = jnp.pad(w, ((0, 0), (0, 0), (0, LANES - QK_ROPE_DIM)))
    return w.reshape(k, n_groups * LANES)


def kernel(x, c, positions, w_ada, b_ada, g_pre_ffn1, w1_gate, w1_up, w1_down, g_post_ffn1, g_pre_mix, w_in, w_dw, b_dw, ln_conv_g, ln_conv_b, g_q_lat, w_uq, g_kv_lat, w_uk, w_uv, g_conv_out, g_attn_out, w_out, g_post_mix, g_pre_ffn2, w2_gate, w2_up, w2_down, g_post_ffn2):
    bsz, seq, d = x.shape
    n = bsz * seq
    depth = w_ada.shape[0]
    half = QK_ROPE_DIM // 2
    inv_freq = ROPE_BASE ** (-jnp.arange(half, dtype=F32) / half)
    invf = jnp.concatenate([inv_freq, inv_freq, jnp.zeros((LANES - QK_ROPE_DIM,), F32)])[None, :]
    pos = positions.reshape(n, 1)
    bf = lambda a: a.astype(BF16)

    xf = x.reshape(n, d)
    for l in range(depth):
        mod = _adaln(c, w_ada[l], b_ada[l][None, :]).reshape(bsz, 3 * N_SUBLAYERS, d)

        xf = _ffn(xf, mod, g_pre_ffn1[l][None], g_post_ffn1[l][None],
                  bf(w1_gate[l]), bf(w1_up[l]), bf(w1_down[l]), sub=0, seq=seq)

        i1 = 2 * D_CONV
        i2 = i1 + Q_LORA_RANK
        i3 = i2 + KV_LORA_RANK
        wi = w_in[l]
        cv, u = _conv_branch(xf.reshape(bsz, seq, d), mod, g_pre_mix[l][None], bf(wi[:, :i1]),
                             jnp.broadcast_to(w_dw[l][:, None, :], (CONV_WIDTH, SUBLANES, D_CONV)),
                             b_dw[l][None], ln_conv_g[l][None], ln_conv_b[l][None], g_conv_out[l][None])

        qk = QK_NOPE_DIM + QK_ROPE_DIM
        wqn = w_uq[l].reshape(Q_LORA_RANK, N_HEADS, qk)[:, :, :QK_NOPE_DIM].reshape(Q_LORA_RANK, -1)
        wqr = _rope_columns(w_uq[l], N_HEADS, qk, QK_NOPE_DIM)
        wkr = _rope_columns(wi[:, i3:], 1, QK_ROPE_DIM, 0)
        qn, qr, kn, kr, v = _mla_proj(
            u.reshape(n, d), pos, invf, bf(wi[:, i1:i2]), bf(wi[:, i2:i3]), bf(wkr),
            g_q_lat[l][None], g_kv_lat[l][None], bf(wqn), bf(wqr), bf(w_uk[l]), bf(w_uv[l]))

        shp = lambda a: a.reshape(bsz, seq, a.shape[-1])
        at, w2g, w2u, w2d, wo = _attention(shp(qn), shp(qr), shp(kn), shp(kr), shp(v),
                                           cast=(w2_gate[l], w2_up[l], w2_down[l], w_out[l]))

        xf = _mix_out(xf, mod, cv.reshape(n, D_CONV), at.reshape(n, D_ATTN), g_attn_out[l][None],
                      wo, g_post_mix[l][None], seq=seq)

        xf = _ffn(xf, mod, g_pre_ffn2[l][None], g_post_ffn2[l][None], w2g, w2u, w2d, sub=2, seq=seq)
    return xf.reshape(bsz, seq, d)
```

```python
import functools

import jax
import jax.numpy as jnp
from jax import lax
from jax.experimental import pallas as pl
from jax.experimental.pallas import tpu as pltpu

D_MODEL = 2048
D_CONV = 1024
CONV_WIDTH = 31
N_HEADS = 8
QK_NOPE_DIM = 128
QK_ROPE_DIM = 64
V_HEAD_DIM = 128
Q_LORA_RANK = 768
KV_LORA_RANK = 512
D_ATTN = N_HEADS * V_HEAD_DIM
D_FF = 5632
FFN_RES_WEIGHT = 0.5
N_SUBLAYERS = 3
ROPE_BASE = 10000.0
EPS = 1e-6
LOG2_E = 1.4426950408889634

LANES = 128
SUBLANES = 8
VMEM_LIMIT = 56 * 1024 * 1024

TM = 512
TM_FFN = 1024
TF = 512
TN_ADA = 1024
TQ = 512
ATTN_HEADS = 2
HALO = 32
CONV_CB = 256
CONV_R = 64
ROW_CH = 16

F32 = jnp.float32
BF16 = jnp.bfloat16


def _rms(x, g):
    return x * lax.rsqrt(jnp.mean(x * x, axis=-1, keepdims=True) + EPS) * g


def _rep(row):
    return jnp.broadcast_to(row, (SUBLANES, row.shape[-1]))


def _unit_rms(x):
    r, d = x.shape
    x = x.reshape(r // SUBLANES, SUBLANES, d)
    return x * lax.rsqrt(jnp.mean(x * x, axis=-1, keepdims=True) + EPS)


def _prenorm_rows(g_ref, mod_ref, sub):
    shift = mod_ref[0, 3 * sub:3 * sub + 1, :]
    scale = mod_ref[0, 3 * sub + 1:3 * sub + 2, :]
    return _rep(g_ref[...] * (1.0 + scale)), _rep(shift)


def _dot(a, b):
    return jnp.dot(a, b, preferred_element_type=F32)


def _row_chunks(n_rows):
    return [slice(r, r + ROW_CH) for r in range(0, n_rows, ROW_CH)]


def _ordering_zero(v, shape, dtype):
    rows, lanes = v.shape
    t = jnp.sum(v.reshape(rows // SUBLANES, SUBLANES, lanes), axis=0)
    bits = lax.shift_right_logical(lax.shift_right_logical(pltpu.bitcast(t, jnp.int32), 16), 16)
    return jnp.tile(bits.astype(F32), (shape[0] // SUBLANES, shape[1] // lanes)).astype(dtype)


def _adaln_kernel(c_ref, w_ref, b_ref, o_ref):
    c = c_ref[...]
    sc = (c * jax.nn.sigmoid(c)).astype(BF16)
    o_ref[...] = _dot(sc, w_ref[...].astype(BF16)) + b_ref[...]


def _adaln(c, w, b):
    bsz, d = c.shape
    n = w.shape[1]
    return pl.pallas_call(
        _adaln_kernel,
        grid=(n // TN_ADA,),
        in_specs=[pl.BlockSpec((bsz, d), lambda j: (0, 0)),
                  pl.BlockSpec((d, TN_ADA), lambda j: (0, j)),
                  pl.BlockSpec((1, TN_ADA), lambda j: (0, j))],
        out_specs=pl.BlockSpec((bsz, TN_ADA), lambda j: (0, j)),
        out_shape=jax.ShapeDtypeStruct((bsz, n), F32),
        compiler_params=pltpu.CompilerParams(dimension_semantics=("arbitrary",),
                                             vmem_limit_bytes=VMEM_LIMIT),
        name="adaln_mod",
    )(c, w, b)


def _ffn_kernel(x_ref, mod_ref, gpre_ref, gpost_ref, wg_ref, wu_ref, wd_ref, o_ref, u_ref, *, sub):
    j = pl.program_id(1)

    def swiglu_block():
        u = u_ref[...]
        g = _dot(u, wg_ref[...])
        up = _dot(u, wu_ref[...])
        h = (g * jax.nn.sigmoid(g) * up).astype(BF16)
        return _dot(h, wd_ref[...])

    @pl.when(j == 0)
    def _():
        gain, shift = _prenorm_rows(gpre_ref, mod_ref, sub)
        for rows in _row_chunks(TM_FFN):
            u = _unit_rms(x_ref[rows, :]) * gain + shift
            u_ref[rows, :] = u.reshape(ROW_CH, -1).astype(BF16)
        o_ref[...] = swiglu_block()

    @pl.when(j > 0)
    def _():
        o_ref[...] += swiglu_block()

    @pl.when(j == pl.num_programs(1) - 1)
    def _():
        gate = _rep(FFN_RES_WEIGHT * mod_ref[0, 3 * sub + 2:3 * sub + 3, :] * gpost_ref[...])
        for rows in _row_chunks(TM_FFN):
            y = (_unit_rms(o_ref[rows, :]) * gate).reshape(ROW_CH, -1)
            o_ref[rows, :] = x_ref[rows, :] + y


def _ffn(x, mod, g_pre, g_post, wg, wu, wd, *, sub, seq):
    n, d = x.shape
    tiles_per_seq = seq // TM_FFN
    row = lambda i, j: (i, 0)
    const = lambda i, j: (0, 0)
    return pl.pallas_call(
        functools.partial(_ffn_kernel, sub=sub),
        grid=(n // TM_FFN, D_FF // TF),
        in_specs=[pl.BlockSpec((TM_FFN, d), row),
                  pl.BlockSpec((1, 3 * N_SUBLAYERS, d), lambda i, j: (i // tiles_per_seq, 0, 0)),
                  pl.BlockSpec((1, d), const),
                  pl.BlockSpec((1, d), const),
                  pl.BlockSpec((d, TF), lambda i, j: (0, j)),
                  pl.BlockSpec((d, TF), lambda i, j: (0, j)),
                  pl.BlockSpec((TF, d), lambda i, j: (j, 0))],
        out_specs=pl.BlockSpec((TM_FFN, d), row),
        out_shape=jax.ShapeDtypeStruct((n, d), F32),
        scratch_shapes=[pltpu.VMEM((TM_FFN, d), BF16)],
        compiler_params=pltpu.CompilerParams(dimension_semantics=("parallel", "arbitrary"),
                                             vmem_limit_bytes=VMEM_LIMIT),
        name=f"ffn{sub}",
    )(x, mod, g_pre, g_post, wg, wu, wd)


def _conv_kernel(x_ref, mod_ref, gpre_ref, w_ref, wdw_ref, bdw_ref, lng_ref, lnb_ref, gout_ref,
                 o_ref, u_ref, hp_ref, sh_ref, cv_ref):
    t = pl.program_id(1)

    @pl.when(t == 0)
    def _():
        hp_ref[0:HALO, :] = jnp.zeros((HALO, D_CONV), F32)

    gain, shift = _prenorm_rows(gpre_ref, mod_ref, 1)
    for rows in _row_chunks(TM):
        u = _unit_rms(x_ref[rows, :]) * gain + shift
        u_ref[0, rows, :] = u.reshape(ROW_CH, -1).astype(BF16)
    u = u_ref[0]

    first = HALO - (CONV_WIDTH - 1)
    sh_rows = TM + HALO - SUBLANES
    for cb in range(D_CONV // CONV_CB):
        cols = slice(cb * CONV_CB, (cb + 1) * CONV_CB)
        gcols = slice(D_CONV + cb * CONV_CB, D_CONV + (cb + 1) * CONV_CB)
        val = _dot(u, w_ref[:, cols])
        hp_ref[HALO:HALO + TM, cols] = val * jax.nn.sigmoid(_dot(u, w_ref[:, gcols]))
        for s in range(1, SUBLANES):
            sh_ref[s - 1, :, cols] = hp_ref[s:s + sh_rows, cols]
        for r0 in range(0, TM, CONV_R):
            acc = jnp.zeros((CONV_R // SUBLANES, SUBLANES, CONV_CB), F32)
            for k in range(CONV_WIDTH):
                s, q = (first + k) % SUBLANES, (first + k) // SUBLANES
                start = r0 + q * SUBLANES
                if s == 0:
                    win = hp_ref[start:start + CONV_R, cols]
                else:
                    win = sh_ref[s - 1, start:start + CONV_R, cols]
                win = win.reshape(CONV_R // SUBLANES, SUBLANES, CONV_CB)
                acc = acc + win * wdw_ref[k, :, cols][None]
            cv_ref[r0:r0 + CONV_R, cols] = acc.reshape(CONV_R, CONV_CB)

    hp_ref[0:HALO, :] = hp_ref[TM:TM + HALO, :]

    bias, ln_g, ln_b, g_out = _rep(bdw_ref[...]), _rep(lng_ref[...]), _rep(lnb_ref[...]), _rep(gout_ref[...])
    for rows in _row_chunks(TM):
        cv = cv_ref[rows, :].reshape(ROW_CH // SUBLANES, SUBLANES, D_CONV) + bias
        xc = cv - jnp.mean(cv, axis=-1, keepdims=True)
        var = jnp.mean(xc * xc, axis=-1, keepdims=True)
        y = xc * lax.rsqrt(var + EPS) * ln_g + ln_b
        y = y * jax.nn.sigmoid(y)
        y = y * lax.rsqrt(jnp.mean(y * y, axis=-1, keepdims=True) + EPS) * g_out
        o_ref[0, rows, :] = y.reshape(ROW_CH, D_CONV).astype(BF16)


def _conv_branch(x, mod, g_pre, w_conv, wdw_b, b_dw, ln_g, ln_b, g_out):
    bsz, seq, d = x.shape
    const2 = lambda b, t: (0, 0)
    return pl.pallas_call(
        _conv_kernel,
        grid=(bsz, seq // TM),
        in_specs=[pl.BlockSpec((None, TM, d), lambda b, t: (b, t, 0)),
                  pl.BlockSpec((1, 3 * N_SUBLAYERS, d), lambda b, t: (b, 0, 0)),
                  pl.BlockSpec((1, d), const2),
                  pl.BlockSpec((d, 2 * D_CONV), const2),
                  pl.BlockSpec((CONV_WIDTH, SUBLANES, D_CONV), lambda b, t: (0, 0, 0)),
                  pl.BlockSpec((1, D_CONV), const2),
                  pl.BlockSpec((1, D_CONV), const2),
                  pl.BlockSpec((1, D_CONV), const2),
                  pl.BlockSpec((1, D_CONV), const2)],
        out_specs=[pl.BlockSpec((1, TM, D_CONV), lambda b, t: (b, t, 0)),
                   pl.BlockSpec((1, TM, d), lambda b, t: (b, t, 0))],
        out_shape=[jax.ShapeDtypeStruct((bsz, seq, D_CONV), BF16),
                   jax.ShapeDtypeStruct((bsz, seq, d), BF16)],
        scratch_shapes=[pltpu.VMEM((TM + HALO, D_CONV), F32),
                        pltpu.VMEM((SUBLANES - 1, TM + HALO - SUBLANES, D_CONV), F32),
                        pltpu.VMEM((TM, D_CONV), F32)],
        compiler_params=pltpu.CompilerParams(dimension_semantics=("parallel", "arbitrary"),
                                             vmem_limit_bytes=VMEM_LIMIT),
        name="conv_branch",
    )(x, mod, g_pre, w_conv, wdw_b, b_dw, ln_g, ln_b, g_out)


def _rope(xr, cos, sin_lo, sin_hi):
    return (xr * cos + pltpu.roll(xr, LANES - QK_ROPE_DIM // 2, axis=1) * sin_lo
            + pltpu.roll(xr, QK_ROPE_DIM // 2, axis=1) * sin_hi)


def _mla_proj_kernel(u_ref, pos_ref, invf_ref, wq_ref, wkv_ref, wkr_ref,
                     gq_ref, gkv_ref, wqn_ref, wqr_ref, wuk_ref, wuv_ref,
                     qn_ref, qr_ref, kn_ref, kr_ref, v_ref):
    u = u_ref[...]

    ang = pos_ref[...].astype(F32) * invf_ref[...]
    cos = jnp.cos(ang)
    sin = jnp.sin(ang)
    lane = lax.broadcasted_iota(jnp.int32, ang.shape, 1)
    half = QK_ROPE_DIM // 2
    sin_lo = jnp.where(lane < half, -sin, 0.0)
    sin_hi = jnp.where((lane >= half) & (lane < QK_ROPE_DIM), sin, 0.0)

    scale = (QK_NOPE_DIM + QK_ROPE_DIM) ** -0.5 * LOG2_E
    qlat = _dot(u, wq_ref[...])
    kvlat = _dot(u, wkv_ref[...])
    kr = _dot(u, wkr_ref[...] + _ordering_zero(cos + sin, wkr_ref.shape, BF16))
    qlat = _rms(qlat, gq_ref[...]).astype(BF16)
    qn_ref[...] = (_dot(qlat, wqn_ref[...]) * scale).astype(BF16)
    qr = _dot(qlat, wqr_ref[...])
    ckv = _rms(kvlat, gkv_ref[...]).astype(BF16)
    kn_ref[...] = _dot(ckv, wuk_ref[...]).astype(BF16)
    v_ref[...] = _dot(ckv, wuv_ref[...]).astype(BF16)
    kr_ref[...] = _rope(kr, cos, sin_lo, sin_hi).astype(BF16)
    for h in range(N_HEADS):
        cols = slice(h * LANES, (h + 1) * LANES)
        qr_ref[:, cols] = (_rope(qr[:, cols], cos, sin_lo, sin_hi) * scale).astype(BF16)


def _mla_proj(u, pos, invf, wq, wkv, wkr, gq, gkv, wqn, wqr, wuk, wuv):
    n, d = u.shape
    row = lambda i: (i, 0)
    const = lambda i: (0, 0)
    full = lambda a: pl.BlockSpec(a.shape, const)
    wide = jax.ShapeDtypeStruct((n, N_HEADS * LANES), BF16)
    return pl.pallas_call(
        _mla_proj_kernel,
        grid=(n // TM,),
        in_specs=[pl.BlockSpec((TM, d), row),
                  pl.BlockSpec((TM, 1), row),
                  full(invf), full(wq), full(wkv), full(wkr), full(gq), full(gkv),
                  full(wqn), full(wqr), full(wuk), full(wuv)],
        out_specs=[pl.BlockSpec((TM, N_HEADS * LANES), row),
                   pl.BlockSpec((TM, N_HEADS * LANES), row),
                   pl.BlockSpec((TM, N_HEADS * LANES), row),
                   pl.BlockSpec((TM, LANES), row),
                   pl.BlockSpec((TM, N_HEADS * LANES), row)],
        out_shape=[wide, wide, wide, jax.ShapeDtypeStruct((n, LANES), BF16), wide],
        compiler_params=pltpu.CompilerParams(dimension_semantics=("parallel",),
                                             vmem_limit_bytes=VMEM_LIMIT),
        name="mla_proj",
    )(u, pos, invf, wq, wkv, wkr, gq, gkv, wqn, wqr, wuk, wuv)


def _dot_nt(a, b):
    return lax.dot_general(a, b, (((1,), (1,)), ((), ())), preferred_element_type=F32)


def _attn_kernel(*refs, n_cast):
    qn_ref, qr_ref, kn_ref, kr_ref, v_ref = refs[:5]
    o_ref = refs[5 + n_cast]
    for src, dst in zip(refs[5:5 + n_cast], refs[6 + n_cast:]):
        dst[...] = src[...].astype(BF16)
    seq = o_ref.shape[1]
    ki = lax.broadcasted_iota(jnp.int32, (TQ, TQ), 0)
    qj = lax.broadcasted_iota(jnp.int32, (TQ, TQ), 1)
    v_t = [v_ref[0, :, hh * LANES:(hh + 1) * LANES].astype(F32).T.astype(BF16)
           for hh in range(ATTN_HEADS)]

    def scores(hh, qi):
        hc = slice(hh * LANES, (hh + 1) * LANES)
        diag = slice(qi * TQ, (qi + 1) * TQ)
        past = slice(0, qi * TQ)
        q = jnp.concatenate([qn_ref[0, diag, hc], qr_ref[0, diag, hc]], axis=-1)
        k_d = jnp.concatenate([kn_ref[0, diag, hc], kr_ref[0, diag, :]], axis=-1)
        s_d = jnp.where(ki <= qj, _dot_nt(k_d, q), -jnp.inf)
        if not qi:
            return s_d, None
        k_p = jnp.concatenate([kn_ref[0, past, hc], kr_ref[0, past, :]], axis=-1)
        return s_d, _dot_nt(k_p, q)

    def softmax(s_d, s_p):
        m = jnp.max(s_d, axis=0, keepdims=True)
        if s_p is not None:
            m = jnp.maximum(m, jnp.max(s_p, axis=0, keepdims=True))
        p_d = jnp.exp2(s_d - m)
        l = jnp.sum(p_d, axis=0, keepdims=True)
        if s_p is None:
            return p_d.astype(BF16), None, l
        p_p = jnp.exp2(s_p - m)
        return p_d.astype(BF16), p_p.astype(BF16), l + jnp.sum(p_p, axis=0, keepdims=True)

    def weighted_values(hh, qi, p_d, p_p, l):
        diag = slice(qi * TQ, (qi + 1) * TQ)
        acc = _dot(v_t[hh][:, diag], p_d)
        if p_p is not None:
            acc = acc + _dot(v_t[hh][:, 0:qi * TQ], p_p)
        o_ref[0, diag, hh * LANES:(hh + 1) * LANES] = (acc * (1.0 / l)).T.astype(BF16)

    chains = [(hh, qi) for qi in reversed(range(seq // TQ)) for hh in range(ATTN_HEADS)]
    s_vals, p_vals = {}, {}
    for t in range(len(chains) + 2):
        if t < len(chains):
            s_vals[t] = scores(*chains[t])
        if 0 <= t - 1 < len(chains):
            p_vals[t - 1] = softmax(*s_vals.pop(t - 1))
        if 0 <= t - 2 < len(chains):
            weighted_values(*chains[t - 2], *p_vals.pop(t - 2))


def _cast_row_spec(a, n_steps):
    rows, repeat = a.shape[0], 1
    while rows % (n_steps // repeat) or (rows // (n_steps // repeat)) % (2 * SUBLANES):
        repeat *= 2
    groups = N_HEADS // ATTN_HEADS
    return pl.BlockSpec((rows // (n_steps // repeat), a.shape[1]),
                        lambda b, h: ((b * groups + h) // repeat, 0))


def _attention(qn, qr, kn, kr, v, cast=()):
    bsz, seq, _ = qn.shape
    groups = N_HEADS // ATTN_HEADS
    head = pl.BlockSpec((1, seq, ATTN_HEADS * LANES), lambda b, h: (b, 0, h))
    cast_specs = [_cast_row_spec(a, bsz * groups) for a in cast]
    return pl.pallas_call(
        functools.partial(_attn_kernel, n_cast=len(cast)),
        grid=(bsz, groups),
        in_specs=[head, head, head, pl.BlockSpec((1, seq, LANES), lambda b, h: (b, 0, 0)), head]
                 + cast_specs,
        out_specs=[head] + cast_specs,
        out_shape=[jax.ShapeDtypeStruct((bsz, seq, D_ATTN), BF16)]
                  + [jax.ShapeDtypeStruct(a.shape, BF16) for a in cast],
        compiler_params=pltpu.CompilerParams(dimension_semantics=("arbitrary", "arbitrary"),
                                             vmem_limit_bytes=VMEM_LIMIT),
        name="mla_attn",
    )(qn, qr, kn, kr, v, *cast)


def _mix_out_kernel(x_ref, mod_ref, cv_ref, at_ref, gattn_ref, wc_ref, wa_ref, gpost_ref, o_ref,
                    an_ref, y_ref):
    g_attn = _rep(gattn_ref[...])
    for rows in _row_chunks(TM):
        a = _unit_rms(at_ref[rows, :].astype(F32)) * g_attn
        an_ref[rows, :] = a.reshape(ROW_CH, -1).astype(BF16)
    y_ref[...] = _dot(cv_ref[...], wc_ref[...]) + _dot(an_ref[...], wa_ref[...])
    gate = _rep(mod_ref[0, 5:6, :] * gpost_ref[...])
    for rows in _row_chunks(TM):
        y = (_unit_rms(y_ref[rows, :]) * gate).reshape(ROW_CH, -1)
        o_ref[rows, :] = x_ref[rows, :] + y


def _mix_out(x, mod, cv, at, g_attn, w_out, g_post, *, seq):
    n, d = x.shape
    tiles_per_seq = seq // TM
    row = lambda i: (i, 0)
    const = lambda i: (0, 0)
    return pl.pallas_call(
        _mix_out_kernel,
        grid=(n // TM,),
        in_specs=[pl.BlockSpec((TM, d), row),
                  pl.BlockSpec((1, 3 * N_SUBLAYERS, d), lambda i: (i // tiles_per_seq, 0, 0)),
                  pl.BlockSpec((TM, D_CONV), row),
                  pl.BlockSpec((TM, D_ATTN), row),
                  pl.BlockSpec((1, D_ATTN), const),
                  pl.BlockSpec((D_CONV, d), const),
                  pl.BlockSpec((D_ATTN, d), lambda i: (D_CONV // D_ATTN, 0)),
                  pl.BlockSpec((1, d), const)],
        out_specs=pl.BlockSpec((TM, d), row),
        out_shape=jax.ShapeDtypeStruct((n, d), F32),
        scratch_shapes=[pltpu.VMEM((TM, D_ATTN), BF16), pltpu.VMEM((TM, d), F32)],
        compiler_params=pltpu.CompilerParams(dimension_semantics=("parallel",),
                                             vmem_limit_bytes=VMEM_LIMIT),
        name="mix_out",
    )(x, mod, cv, at, g_attn, w_out, w_out, g_post)


def _rope_columns(w, n_groups, group, lo):
    k = w.shape[0]
    w = w.reshape(k, n_groups, group)[:, :, lo:lo + QK_ROPE_DIM]
    w = jnp.pad(w, ((0, 0), (0, 0), (0, LANES - QK_ROPE_DIM)))
    return w.reshape(k, n_groups * LANES)


def kernel(x, c, positions, w_ada, b_ada, g_pre_ffn1, w1_gate, w1_up, w1_down, g_post_ffn1, g_pre_mix, w_in, w_dw, b_dw, ln_conv_g, ln_conv_b, g_q_lat, w_uq, g_kv_lat, w_uk, w_uv, g_conv_out, g_attn_out, w_out, g_post_mix, g_pre_ffn2, w2_gate, w2_up, w2_down, g_post_ffn2):
    bsz, seq, d = x.shape
    n = bsz * seq
    depth = w_ada.shape[0]
    half = QK_ROPE_DIM // 2
    inv_freq = ROPE_BASE ** (-jnp.arange(half, dtype=F32) / half)
    invf = jnp.concatenate([inv_freq, inv_freq, jnp.zeros((LANES - QK_ROPE_DIM,), F32)])[None, :]
    pos = positions.reshape(n, 1)
    bf = lambda a: a.astype(BF16)

    xf = x.reshape(n, d)
    for l in range(depth):
        mod = _adaln(c, w_ada[l], b_ada[l][None, :]).reshape(bsz, 3 * N_SUBLAYERS, d)

        xf = _ffn(xf, mod, g_pre_ffn1[l][None], g_post_ffn1[l][None],
                  bf(w1_gate[l]), bf(w1_up[l]), bf(w1_down[l]), sub=0, seq=seq)

        i1 = 2 * D_CONV
        i2 = i1 + Q_LORA_RANK
        i3 = i2 + KV_LORA_RANK
        wi = w_in[l]
        cv, u = _conv_branch(xf.reshape(bsz, seq, d), mod, g_pre_mix[l][None], bf(wi[:, :i1]),
                             jnp.broadcast_to(w_dw[l][:, None, :], (CONV_WIDTH, SUBLANES, D_CONV)),
                             b_dw[l][None], ln_conv_g[l][None], ln_conv_b[l][None], g_conv_out[l][None])

        qk = QK_NOPE_DIM + QK_ROPE_DIM
        wqn = w_uq[l].reshape(Q_LORA_RANK, N_HEADS, qk)[:, :, :QK_NOPE_DIM].reshape(Q_LORA_RANK, -1)
        wqr = _rope_columns(w_uq[l], N_HEADS, qk, QK_NOPE_DIM)
        wkr = _rope_columns(wi[:, i3:], 1, QK_ROPE_DIM, 0)
        qn, qr, kn, kr, v = _mla_proj(
            u.reshape(n, d), pos, invf, bf(wi[:, i1:i2]), bf(wi[:, i2:i3]), bf(wkr),
            g_q_lat[l][None], g_kv_lat[l][None], bf(wqn), bf(wqr), bf(w_uk[l]), bf(w_uv[l]))

        shp = lambda a: a.reshape(bsz, seq, a.shape[-1])
        at, w2g, w2u, w2d, wo = _attention(shp(qn), shp(qr), shp(kn), shp(kr), shp(v),
                                           cast=(w2_gate[l], w2_up[l], w2_down[l], w_out[l]))

        xf = _mix_out(xf, mod, cv.reshape(n, D_CONV), at.reshape(n, D_ATTN), g_attn_out[l][None],
                      wo, g_post_mix[l][None], seq=seq)

        xf = _ffn(xf, mod, g_pre_ffn2[l][None], g_post_ffn2[l][None], w2g, w2u, w2d, sub=2, seq=seq)
    return xf.reshape(bsz, seq, d)
```

```python
import functools

import jax
import jax.numpy as jnp
import numpy as np
from jax import lax
from jax.experimental import pallas as pl
from jax.experimental.pallas import tpu as pltpu

D_MODEL = 2048
D_CONV = 1024
CONV_WIDTH = 31
N_HEADS = 8
QK_NOPE_DIM = 128
QK_ROPE_DIM = 64
V_HEAD_DIM = 128
Q_LORA_RANK = 768
KV_LORA_RANK = 512
D_ATTN = N_HEADS * V_HEAD_DIM
D_FF = 5632
FFN_RES_WEIGHT = 0.5
N_SUBLAYERS = 3
ROPE_BASE = 10000.0
EPS = 1e-6
LOG2_E = 1.4426950408889634

LANES = 128
SUBLANES = 8
VMEM_LIMIT = 56 * 1024 * 1024

TM = 512
TM_FFN = 1024
TF = 512
TN_ADA = 1024
TQ = 512
ATTN_HEADS = 2
HALO = 32
CONV_CB = 256
CONV_RB = 256
CONV_PAD = 2 * SUBLANES
CONV_SEG = CONV_RB + CONV_PAD
CONV_P_CHUNKS = ((0, 96), (96, 96), (192, 80))
ROW_CH = 16

F32 = jnp.float32
BF16 = jnp.bfloat16


def _rms(x, g):
    return x * lax.rsqrt(jnp.mean(x * x, axis=-1, keepdims=True) + EPS) * g


def _rep(row):
    return jnp.broadcast_to(row, (SUBLANES, row.shape[-1]))


def _unit_rms(x):
    r, d = x.shape
    x = x.reshape(r // SUBLANES, SUBLANES, d)
    return x * lax.rsqrt(jnp.mean(x * x, axis=-1, keepdims=True) + EPS)


def _prenorm_rows(g_ref, mod_ref, sub):
    shift = mod_ref[0, 3 * sub:3 * sub + 1, :]
    scale = mod_ref[0, 3 * sub + 1:3 * sub + 2, :]
    return _rep(g_ref[...] * (1.0 + scale)), _rep(shift)


def _dot(a, b):
    return jnp.dot(a, b, preferred_element_type=F32)


def _row_chunks(n_rows):
    return [slice(r, r + ROW_CH) for r in range(0, n_rows, ROW_CH)]


def _ordering_zero(v, shape, dtype):
    rows, lanes = v.shape
    t = jnp.sum(v.reshape(rows // SUBLANES, SUBLANES, lanes), axis=0)
    bits = lax.shift_right_logical(lax.shift_right_logical(pltpu.bitcast(t, jnp.int32), 16), 16)
    return jnp.tile(bits.astype(F32), (shape[0] // SUBLANES, shape[1] // lanes)).astype(dtype)


def _adaln_kernel(c_ref, w_ref, b_ref, o_ref):
    c = c_ref[...]
    sc = (c * jax.nn.sigmoid(c)).astype(BF16)
    o_ref[...] = _dot(sc, w_ref[...].astype(BF16)) + b_ref[...]


def _adaln(c, w, b):
    bsz, d = c.shape
    n = w.shape[1]
    return pl.pallas_call(
        _adaln_kernel,
        grid=(n // TN_ADA,),
        in_specs=[pl.BlockSpec((bsz, d), lambda j: (0, 0)),
                  pl.BlockSpec((d, TN_ADA), lambda j: (0, j)),
                  pl.BlockSpec((1, TN_ADA), lambda j: (0, j))],
        out_specs=pl.BlockSpec((bsz, TN_ADA), lambda j: (0, j)),
        out_shape=jax.ShapeDtypeStruct((bsz, n), F32),
        compiler_params=pltpu.CompilerParams(dimension_semantics=("arbitrary",),
                                             vmem_limit_bytes=VMEM_LIMIT),
        name="adaln_mod",
    )(c, w, b)


def _ffn_kernel(x_ref, mod_ref, gpre_ref, gpost_ref, wg_ref, wu_ref, wd_ref, o_ref, u_ref, *, sub):
    j = pl.program_id(1)

    def swiglu_block():
        u = u_ref[...]
        g = _dot(u, wg_ref[...])
        up = _dot(u, wu_ref[...])
        h = (g * jax.nn.sigmoid(g) * up).astype(BF16)
        return _dot(h, wd_ref[...])

    @pl.when(j == 0)
    def _():
        gain, shift = _prenorm_rows(gpre_ref, mod_ref, sub)
        for rows in _row_chunks(TM_FFN):
            u = _unit_rms(x_ref[rows, :]) * gain + shift
            u_ref[rows, :] = u.reshape(ROW_CH, -1).astype(BF16)
        o_ref[...] = swiglu_block()

    @pl.when(j > 0)
    def _():
        o_ref[...] += swiglu_block()

    @pl.when(j == pl.num_programs(1) - 1)
    def _():
        gate = _rep(FFN_RES_WEIGHT * mod_ref[0, 3 * sub + 2:3 * sub + 3, :] * gpost_ref[...])
        for rows in _row_chunks(TM_FFN):
            y = (_unit_rms(o_ref[rows, :]) * gate).reshape(ROW_CH, -1)
            o_ref[rows, :] = x_ref[rows, :] + y


def _ffn(x, mod, g_pre, g_post, wg, wu, wd, *, sub, seq):
    n, d = x.shape
    tiles_per_seq = seq // TM_FFN
    row = lambda i, j: (i, 0)
    const = lambda i, j: (0, 0)
    return pl.pallas_call(
        functools.partial(_ffn_kernel, sub=sub),
        grid=(n // TM_FFN, D_FF // TF),
        in_specs=[pl.BlockSpec((TM_FFN, d), row),
                  pl.BlockSpec((1, 3 * N_SUBLAYERS, d), lambda i, j: (i // tiles_per_seq, 0, 0)),
                  pl.BlockSpec((1, d), const),
                  pl.BlockSpec((1, d), const),
                  pl.BlockSpec((d, TF), lambda i, j: (0, j)),
                  pl.BlockSpec((d, TF), lambda i, j: (0, j)),
                  pl.BlockSpec((TF, d), lambda i, j: (j, 0))],
        out_specs=pl.BlockSpec((TM_FFN, d), row),
        out_shape=jax.ShapeDtypeStruct((n, d), F32),
        scratch_shapes=[pltpu.VMEM((TM_FFN, d), BF16)],
        compiler_params=pltpu.CompilerParams(dimension_semantics=("parallel", "arbitrary"),
                                             vmem_limit_bytes=VMEM_LIMIT),
        name=f"ffn{sub}",
    )(x, mod, g_pre, g_post, wg, wu, wd)


def _conv_kernel(x_ref, mod_ref, gpre_ref, w_ref, wdw_ref, shift_ref, bdw_ref, lng_ref, lnb_ref, gout_ref,
                 o_ref, u_ref, hp_ref, pa_ref, pb_ref, cv_ref):
    t = pl.program_id(1)

    @pl.when(t == 0)
    def _():
        hp_ref[0:HALO, :] = jnp.zeros((HALO, D_CONV), F32)

    gain, shift = _prenorm_rows(gpre_ref, mod_ref, 1)
    for rows in _row_chunks(TM):
        u = _unit_rms(x_ref[rows, :]) * gain + shift
        u_ref[0, rows, :] = u.reshape(ROW_CH, -1).astype(BF16)
    u = u_ref[0]

    first = HALO - (CONV_WIDTH - 1)
    n_cb = D_CONV // CONV_CB
    hp_ref[TM + HALO:, :] = jnp.zeros((CONV_PAD, D_CONV), F32)
    phase_taps = [[(k, (first + k) // SUBLANES) for k in range(CONV_WIDTH)
                   if (first + k) % SUBLANES == s] for s in range(SUBLANES)]
    p_bufs = (pa_ref, pb_ref)
    n_block = 0

    def project(cb):
        vg = _dot(u, w_ref[:, 2 * cb * CONV_CB:2 * (cb + 1) * CONV_CB])
        return vg[:, :CONV_CB], vg[:, CONV_CB:]

    ahead = project(0)
    for cb in range(n_cb):
        cols = slice(cb * CONV_CB, (cb + 1) * CONV_CB)
        val, gate = ahead
        hp_ref[HALO:HALO + TM, cols] = val * jax.nn.sigmoid(gate)
        for r0 in range(0, TM, CONV_RB):
            if r0 == CONV_RB and cb + 1 < n_cb:
                ahead = project(cb + 1)
            p_ref = p_bufs[n_block % 2]
            n_block += 1
            for s in range(SUBLANES):
                for c0, rows in CONV_P_CHUNKS:
                    acc = None
                    for k, q in phase_taps[s]:
                        start = r0 + q * SUBLANES + c0
                        win = hp_ref[start:start + rows, cols].reshape(rows // SUBLANES, SUBLANES, CONV_CB)
                        term = win * wdw_ref[k, :, cols][None]
                        acc = term if acc is None else acc + term
                    lo = s * CONV_SEG + c0
                    p_ref[lo:lo + rows, :] = acc.reshape(rows, CONV_CB).astype(BF16)
            cv_ref[r0:r0 + CONV_RB, cols] = _dot(shift_ref[...], p_ref[...])

    hp_ref[0:HALO, :] = hp_ref[TM:TM + HALO, :]

    bias, ln_g, ln_b, g_out = _rep(bdw_ref[...]), _rep(lng_ref[...]), _rep(lnb_ref[...]), _rep(gout_ref[...])
    for rows in _row_chunks(TM):
        cv = cv_ref[rows, :].reshape(ROW_CH // SUBLANES, SUBLANES, D_CONV) + bias
        xc = cv - jnp.mean(cv, axis=-1, keepdims=True)
        var = jnp.mean(xc * xc, axis=-1, keepdims=True)
        y = xc * lax.rsqrt(var + EPS) * ln_g + ln_b
        y = y * jax.nn.sigmoid(y)
        y = y * lax.rsqrt(jnp.mean(y * y, axis=-1, keepdims=True) + EPS) * g_out
        o_ref[0, rows, :] = y.reshape(ROW_CH, D_CONV).astype(BF16)


def _conv_shift_matrix():
    m = np.zeros((CONV_RB, SUBLANES * CONV_SEG), np.float32)
    i = np.arange(CONV_RB)
    for s in range(SUBLANES):
        m[i, s * CONV_SEG + i + s] = 1.0
    return jnp.asarray(m, BF16)


def _conv_branch(x, mod, g_pre, w_conv, wdw_b, b_dw, ln_g, ln_b, g_out):
    bsz, seq, d = x.shape
    const2 = lambda b, t: (0, 0)
    return pl.pallas_call(
        _conv_kernel,
        grid=(bsz, seq // TM),
        in_specs=[pl.BlockSpec((None, TM, d), lambda b, t: (b, t, 0)),
                  pl.BlockSpec((1, 3 * N_SUBLAYERS, d), lambda b, t: (b, 0, 0)),
                  pl.BlockSpec((1, d), const2),
                  pl.BlockSpec((d, 2 * D_CONV), const2),
                  pl.BlockSpec((CONV_WIDTH, SUBLANES, D_CONV), lambda b, t: (0, 0, 0)),
                  pl.BlockSpec((CONV_RB, SUBLANES * CONV_SEG), const2),
                  pl.BlockSpec((1, D_CONV), const2),
                  pl.BlockSpec((1, D_CONV), const2),
                  pl.BlockSpec((1, D_CONV), const2),
                  pl.BlockSpec((1, D_CONV), const2)],
        out_specs=[pl.BlockSpec((1, TM, D_CONV), lambda b, t: (b, t, 0)),
                   pl.BlockSpec((1, TM, d), lambda b, t: (b, t, 0))],
        out_shape=[jax.ShapeDtypeStruct((bsz, seq, D_CONV), BF16),
                   jax.ShapeDtypeStruct((bsz, seq, d), BF16)],
        scratch_shapes=[pltpu.VMEM((TM + HALO + CONV_PAD, D_CONV), F32),
                        pltpu.VMEM((SUBLANES * CONV_SEG, CONV_CB), BF16),
                        pltpu.VMEM((SUBLANES * CONV_SEG, CONV_CB), BF16),
                        pltpu.VMEM((TM, D_CONV), F32)],
        compiler_params=pltpu.CompilerParams(dimension_semantics=("parallel", "arbitrary"),
                                             vmem_limit_bytes=VMEM_LIMIT),
        name="conv_branch",
    )(x, mod, g_pre, w_conv, wdw_b, _conv_shift_matrix(), b_dw, ln_g, ln_b, g_out)


def _rope(xr, cos, sin_lo, sin_hi):
    return (xr * cos + pltpu.roll(xr, LANES - QK_ROPE_DIM // 2, axis=1) * sin_lo
            + pltpu.roll(xr, QK_ROPE_DIM // 2, axis=1) * sin_hi)


def _mla_proj_kernel(u_ref, pos_ref, invf_ref, wq_ref, wkv_ref, wkr_ref,
                     gq_ref, gkv_ref, wqn_ref, wqr_ref, wuk_ref, wuv_ref,
                     qn_ref, qr_ref, kn_ref, kr_ref, v_ref):
    u = u_ref[...]

    ang = pos_ref[...].astype(F32) * invf_ref[...]
    cos = jnp.cos(ang)
    sin = jnp.sin(ang)
    lane = lax.broadcasted_iota(jnp.int32, ang.shape, 1)
    half = QK_ROPE_DIM // 2
    sin_lo = jnp.where(lane < half, -sin, 0.0)
    sin_hi = jnp.where((lane >= half) & (lane < QK_ROPE_DIM), sin, 0.0)

    scale = (QK_NOPE_DIM + QK_ROPE_DIM) ** -0.5 * LOG2_E
    qlat = _dot(u, wq_ref[...])
    kvlat = _dot(u, wkv_ref[...])
    kr = _dot(u, wkr_ref[...] + _ordering_zero(cos + sin, wkr_ref.shape, BF16))
    qlat = _rms(qlat, gq_ref[...]).astype(BF16)
    qn_ref[...] = (_dot(qlat, wqn_ref[...]) * scale).astype(BF16)
    qr = _dot(qlat, wqr_ref[...])
    ckv = _rms(kvlat, gkv_ref[...]).astype(BF16)
    kn_ref[...] = _dot(ckv, wuk_ref[...]).astype(BF16)
    v_ref[...] = _dot(ckv, wuv_ref[...]).astype(BF16)
    kr_ref[...] = _rope(kr, cos, sin_lo, sin_hi).astype(BF16)
    for h in range(N_HEADS):
        cols = slice(h * LANES, (h + 1) * LANES)
        qr_ref[:, cols] = (_rope(qr[:, cols], cos, sin_lo, sin_hi) * scale).astype(BF16)


def _mla_proj(u, pos, invf, wq, wkv, wkr, gq, gkv, wqn, wqr, wuk, wuv):
    n, d = u.shape
    row = lambda i: (i, 0)
    const = lambda i: (0, 0)
    full = lambda a: pl.BlockSpec(a.shape, const)
    wide = jax.ShapeDtypeStruct((n, N_HEADS * LANES), BF16)
    return pl.pallas_call(
        _mla_proj_kernel,
        grid=(n // TM,),
        in_specs=[pl.BlockSpec((TM, d), row),
                  pl.BlockSpec((TM, 1), row),
                  full(invf), full(wq), full(wkv), full(wkr), full(gq), full(gkv),
                  full(wqn), full(wqr), full(wuk), full(wuv)],
        out_specs=[pl.BlockSpec((TM, N_HEADS * LANES), row),
                   pl.BlockSpec((TM, N_HEADS * LANES), row),
                   pl.BlockSpec((TM, N_HEADS * LANES), row),
                   pl.BlockSpec((TM, LANES), row),
                   pl.BlockSpec((TM, N_HEADS * LANES), row)],
        out_shape=[wide, wide, wide, jax.ShapeDtypeStruct((n, LANES), BF16), wide],
        compiler_params=pltpu.CompilerParams(dimension_semantics=("parallel",),
                                             vmem_limit_bytes=VMEM_LIMIT),
        name="mla_proj",
    )(u, pos, invf, wq, wkv, wkr, gq, gkv, wqn, wqr, wuk, wuv)


def _dot_nt(a, b):
    return lax.dot_general(a, b, (((1,), (1,)), ((), ())), preferred_element_type=F32)


def _attn_kernel(*refs, n_cast):
    qn_ref, qr_ref, kn_ref, kr_ref, v_ref = refs[:5]
    o_ref = refs[5 + n_cast]
    for src, dst in zip(refs[5:5 + n_cast], refs[6 + n_cast:]):
        dst[...] = src[...].astype(BF16)
    seq = o_ref.shape[1]
    ki = lax.broadcasted_iota(jnp.int32, (TQ, TQ), 0)
    qj = lax.broadcasted_iota(jnp.int32, (TQ, TQ), 1)
    v_t = [v_ref[0, :, hh * LANES:(hh + 1) * LANES].astype(F32).T.astype(BF16)
           for hh in range(ATTN_HEADS)]

    def scores(hh, qi):
        hc = slice(hh * LANES, (hh + 1) * LANES)
        diag = slice(qi * TQ, (qi + 1) * TQ)
        past = slice(0, qi * TQ)
        q = jnp.concatenate([qn_ref[0, diag, hc], qr_ref[0, diag, hc]], axis=-1)
        k_d = jnp.concatenate([kn_ref[0, diag, hc], kr_ref[0, diag, :]], axis=-1)
        s_d = jnp.where(ki <= qj, _dot_nt(k_d, q), -jnp.inf)
        if not qi:
            return s_d, None
        k_p = jnp.concatenate([kn_ref[0, past, hc], kr_ref[0, past, :]], axis=-1)
        return s_d, _dot_nt(k_p, q)

    def softmax(s_d, s_p):
        m = jnp.max(s_d, axis=0, keepdims=True)
        if s_p is not None:
            m = jnp.maximum(m, jnp.max(s_p, axis=0, keepdims=True))
        p_d = jnp.exp2(s_d - m)
        l = jnp.sum(p_d, axis=0, keepdims=True)
        if s_p is None:
            return p_d.astype(BF16), None, l
        p_p = jnp.exp2(s_p - m)
        return p_d.astype(BF16), p_p.astype(BF16), l + jnp.sum(p_p, axis=0, keepdims=True)

    def weighted_values(hh, qi, p_d, p_p, l):
        diag = slice(qi * TQ, (qi + 1) * TQ)
        acc = _dot(v_t[hh][:, diag], p_d)
        if p_p is not None:
            acc = acc + _dot(v_t[hh][:, 0:qi * TQ], p_p)
        o_ref[0, diag, hh * LANES:(hh + 1) * LANES] = (acc * (1.0 / l)).T.astype(BF16)

    chains = [(hh, qi) for qi in reversed(range(seq // TQ)) for hh in range(ATTN_HEADS)]
    s_vals, p_vals = {}, {}
    for t in range(len(chains) + 2):
        if t < len(chains):
            s_vals[t] = scores(*chains[t])
        if 0 <= t - 1 < len(chains):
            p_vals[t - 1] = softmax(*s_vals.pop(t - 1))
        if 0 <= t - 2 < len(chains):
            weighted_values(*chains[t - 2], *p_vals.pop(t - 2))


def _cast_row_spec(a, n_steps):
    rows, repeat = a.shape[0], 1
    while rows % (n_steps // repeat) or (rows // (n_steps // repeat)) % (2 * SUBLANES):
        repeat *= 2
    groups = N_HEADS // ATTN_HEADS
    return pl.BlockSpec((rows // (n_steps // repeat), a.shape[1]),
                        lambda b, h: ((b * groups + h) // repeat, 0))


def _attention(qn, qr, kn, kr, v, cast=()):
    bsz, seq, _ = qn.shape
    groups = N_HEADS // ATTN_HEADS
    head = pl.BlockSpec((1, seq, ATTN_HEADS * LANES), lambda b, h: (b, 0, h))
    cast_specs = [_cast_row_spec(a, bsz * groups) for a in cast]
    return pl.pallas_call(
        functools.partial(_attn_kernel, n_cast=len(cast)),
        grid=(bsz, groups),
        in_specs=[head, head, head, pl.BlockSpec((1, seq, LANES), lambda b, h: (b, 0, 0)), head]
                 + cast_specs,
        out_specs=[head] + cast_specs,
        out_shape=[jax.ShapeDtypeStruct((bsz, seq, D_ATTN), BF16)]
                  + [jax.ShapeDtypeStruct(a.shape, BF16) for a in cast],
        compiler_params=pltpu.CompilerParams(dimension_semantics=("arbitrary", "arbitrary"),
                                             vmem_limit_bytes=VMEM_LIMIT),
        name="mla_attn",
    )(qn, qr, kn, kr, v, *cast)


def _mix_out_kernel(x_ref, mod_ref, cv_ref, at_ref, gattn_ref, wc_ref, wa_ref, gpost_ref, o_ref,
                    an_ref, y_ref):
    g_attn = _rep(gattn_ref[...])
    for rows in _row_chunks(TM):
        a = _unit_rms(at_ref[rows, :].astype(F32)) * g_attn
        an_ref[rows, :] = a.reshape(ROW_CH, -1).astype(BF16)
    y_ref[...] = _dot(cv_ref[...], wc_ref[...]) + _dot(an_ref[...], wa_ref[...])
    gate = _rep(mod_ref[0, 5:6, :] * gpost_ref[...])
    for rows in _row_chunks(TM):
        y = (_unit_rms(y_ref[rows, :]) * gate).reshape(ROW_CH, -1)
        o_ref[rows, :] = x_ref[rows, :] + y


def _mix_out(x, mod, cv, at, g_attn, w_out, g_post, *, seq):
    n, d = x.shape
    tiles_per_seq = seq // TM
    row = lambda i: (i, 0)
    const = lambda i: (0, 0)
    return pl.pallas_call(
        _mix_out_kernel,
        grid=(n // TM,),
        in_specs=[pl.BlockSpec((TM, d), row),
                  pl.BlockSpec((1, 3 * N_SUBLAYERS, d), lambda i: (i // tiles_per_seq, 0, 0)),
                  pl.BlockSpec((TM, D_CONV), row),
                  pl.BlockSpec((TM, D_ATTN), row),
                  pl.BlockSpec((1, D_ATTN), const),
                  pl.BlockSpec((D_CONV, d), const),
                  pl.BlockSpec((D_ATTN, d), lambda i: (D_CONV // D_ATTN, 0)),
                  pl.BlockSpec((1, d), const)],
        out_specs=pl.BlockSpec((TM, d), row),
        out_shape=jax.ShapeDtypeStruct((n, d), F32),
        scratch_shapes=[pltpu.VMEM((TM, D_ATTN), BF16), pltpu.VMEM((TM, d), F32)],
        compiler_params=pltpu.CompilerParams(dimension_semantics=("parallel",),
                                             vmem_limit_bytes=VMEM_LIMIT),
        name="mix_out",
    )(x, mod, cv, at, g_attn, w_out, w_out, g_post)


def _rope_columns(w, n_groups, group, lo):
    k = w.shape[0]
    w = w.reshape(k, n_groups, group)[:, :, lo:lo + QK_ROPE_DIM]
    w = jnp.pad(w, ((0, 0), (0, 0), (0, LANES - QK_ROPE_DIM)))
    return w.reshape(k, n_groups * LANES)


def kernel(x, c, positions, w_ada, b_ada, g_pre_ffn1, w1_gate, w1_up, w1_down, g_post_ffn1, g_pre_mix, w_in, w_dw, b_dw, ln_conv_g, ln_conv_b, g_q_lat, w_uq, g_kv_lat, w_uk, w_uv, g_conv_out, g_attn_out, w_out, g_post_mix, g_pre_ffn2, w2_gate, w2_up, w2_down, g_post_ffn2):
    bsz, seq, d = x.shape
    n = bsz * seq
    depth = w_ada.shape[0]
    half = QK_ROPE_DIM // 2
    inv_freq = ROPE_BASE ** (-jnp.arange(half, dtype=F32) / half)
    invf = jnp.concatenate([inv_freq, inv_freq, jnp.zeros((LANES - QK_ROPE_DIM,), F32)])[None, :]
    pos = positions.reshape(n, 1)
    bf = lambda a: a.astype(BF16)

    xf = x.reshape(n, d)
    for l in range(depth):
        mod = _adaln(c, w_ada[l], b_ada[l][None, :]).reshape(bsz, 3 * N_SUBLAYERS, d)

        xf = _ffn(xf, mod, g_pre_ffn1[l][None], g_post_ffn1[l][None],
                  bf(w1_gate[l]), bf(w1_up[l]), bf(w1_down[l]), sub=0, seq=seq)

        i1 = 2 * D_CONV
        i2 = i1 + Q_LORA_RANK
        i3 = i2 + KV_LORA_RANK
        wi = w_in[l]
        w_conv = wi[:, :i1].reshape(d, 2, D_CONV // CONV_CB, CONV_CB).transpose(0, 2, 1, 3).reshape(d, i1)
        cv, u = _conv_branch(xf.reshape(bsz, seq, d), mod, g_pre_mix[l][None], bf(w_conv),
                             jnp.broadcast_to(w_dw[l][:, None, :], (CONV_WIDTH, SUBLANES, D_CONV)),
                             b_dw[l][None], ln_conv_g[l][None], ln_conv_b[l][None], g_conv_out[l][None])

        qk = QK_NOPE_DIM + QK_ROPE_DIM
        wqn = w_uq[l].reshape(Q_LORA_RANK, N_HEADS, qk)[:, :, :QK_NOPE_DIM].reshape(Q_LORA_RANK, -1)
        wqr = _rope_columns(w_uq[l], N_HEADS, qk, QK_NOPE_DIM)
        wkr = _rope_columns(wi[:, i3:], 1, QK_ROPE_DIM, 0)
        qn, qr, kn, kr, v = _mla_proj(
            u.reshape(n, d), pos, invf, bf(wi[:, i1:i2]), bf(wi[:, i2:i3]), bf(wkr),
            g_q_lat[l][None], g_kv_lat[l][None], bf(wqn), bf(wqr), bf(w_uk[l]), bf(w_uv[l]))

        shp = lambda a: a.reshape(bsz, seq, a.shape[-1])
        at, w2g, w2u, w2d, wo = _attention(shp(qn), shp(qr), shp(kn), shp(kr), shp(v),
                                           cast=(w2_gate[l], w2_up[l], w2_down[l], w_out[l]))

        xf = _mix_out(xf, mod, cv.reshape(n, D_CONV), at.reshape(n, D_ATTN), g_attn_out[l][None],
                      wo, g_post_mix[l][None], seq=seq)

        xf = _ffn(xf, mod, g_pre_ffn2[l][None], g_post_ffn2[l][None], w2g, w2u, w2d, sub=2, seq=seq)
    return xf.reshape(bsz, seq, d)
```

```python
import functools

import jax
import jax.numpy as jnp
from jax import lax
from jax.experimental import pallas as pl
from jax.experimental.pallas import tpu as pltpu

D_MODEL = 2048
D_CONV = 1024
CONV_WIDTH = 31
N_HEADS = 8
QK_NOPE_DIM = 128
QK_ROPE_DIM = 64
V_HEAD_DIM = 128
Q_LORA_RANK = 768
KV_LORA_RANK = 512
D_ATTN = N_HEADS * V_HEAD_DIM
D_FF = 5632
FFN_RES_WEIGHT = 0.5
N_SUBLAYERS = 3
ROPE_BASE = 10000.0
EPS = 1e-6
LOG2_E = 1.4426950408889634

LANES = 128
SUBLANES = 8
VMEM_LIMIT = 56 * 1024 * 1024

TM = 512
TM_FFN = 1024
TF = 512
TN_ADA = 1024
TQ = 512
ATTN_HEADS = 2
MLA_WB = 256
HALO = 32
CONV_CB = 256
CONV_R = 64
ROW_CH = 16

F32 = jnp.float32
BF16 = jnp.bfloat16


def _rms(x, g):
    return x * lax.rsqrt(jnp.mean(x * x, axis=-1, keepdims=True) + EPS) * g


def _rep(row):
    return jnp.broadcast_to(row, (SUBLANES, row.shape[-1]))


def _unit_rms(x):
    r, d = x.shape
    x = x.reshape(r // SUBLANES, SUBLANES, d)
    return x * lax.rsqrt(jnp.mean(x * x, axis=-1, keepdims=True) + EPS)


def _prenorm_rows(g_ref, mod_ref, sub):
    shift = mod_ref[0, 3 * sub:3 * sub + 1, :]
    scale = mod_ref[0, 3 * sub + 1:3 * sub + 2, :]
    return _rep(g_ref[...] * (1.0 + scale)), _rep(shift)


def _dot(a, b):
    return jnp.dot(a, b, preferred_element_type=F32)


def _row_chunks(n_rows):
    return [slice(r, r + ROW_CH) for r in range(0, n_rows, ROW_CH)]


def _ordering_zero(v, shape, dtype):
    rows, lanes = v.shape
    t = jnp.sum(v.reshape(rows // SUBLANES, SUBLANES, lanes), axis=0)
    bits = lax.shift_right_logical(lax.shift_right_logical(pltpu.bitcast(t, jnp.int32), 16), 16)
    return jnp.tile(bits.astype(F32), (shape[0] // SUBLANES, shape[1] // lanes)).astype(dtype)


def _adaln_kernel(c_ref, w_ref, b_ref, o_ref):
    c = c_ref[...]
    sc = (c * jax.nn.sigmoid(c)).astype(BF16)
    o_ref[...] = _dot(sc, w_ref[...].astype(BF16)) + b_ref[...]


def _adaln(c, w, b):
    bsz, d = c.shape
    n = w.shape[1]
    return pl.pallas_call(
        _adaln_kernel,
        grid=(n // TN_ADA,),
        in_specs=[pl.BlockSpec((bsz, d), lambda j: (0, 0)),
                  pl.BlockSpec((d, TN_ADA), lambda j: (0, j)),
                  pl.BlockSpec((1, TN_ADA), lambda j: (0, j))],
        out_specs=pl.BlockSpec((bsz, TN_ADA), lambda j: (0, j)),
        out_shape=jax.ShapeDtypeStruct((bsz, n), F32),
        compiler_params=pltpu.CompilerParams(dimension_semantics=("arbitrary",),
                                             vmem_limit_bytes=VMEM_LIMIT),
        name="adaln_mod",
    )(c, w, b)


def _ffn_kernel(x_ref, mod_ref, gpre_ref, gpost_ref, wg_ref, wu_ref, wd_ref, o_ref, u_ref, *, sub):
    j = pl.program_id(1)

    def swiglu_block():
        u = u_ref[...]
        g = _dot(u, wg_ref[...])
        up = _dot(u, wu_ref[...])
        h = (g * jax.nn.sigmoid(g) * up).astype(BF16)
        return _dot(h, wd_ref[...])

    @pl.when(j == 0)
    def _():
        gain, shift = _prenorm_rows(gpre_ref, mod_ref, sub)
        for rows in _row_chunks(TM_FFN):
            u = _unit_rms(x_ref[rows, :]) * gain + shift
            u_ref[rows, :] = u.reshape(ROW_CH, -1).astype(BF16)
        o_ref[...] = swiglu_block()

    last = pl.num_programs(1) - 1

    @pl.when((j > 0) & (j < last))
    def _():
        o_ref[...] += swiglu_block()

    @pl.when(j == last)
    def _():
        o_ref[...] += swiglu_block()
        gate = _rep(FFN_RES_WEIGHT * mod_ref[0, 3 * sub + 2:3 * sub + 3, :] * gpost_ref[...])
        for rows in _row_chunks(TM_FFN):
            y = (_unit_rms(o_ref[rows, :]) * gate).reshape(ROW_CH, -1)
            o_ref[rows, :] = x_ref[rows, :] + y


def _ffn(x, mod, g_pre, g_post, wg, wu, wd, *, sub, seq):
    n, d = x.shape
    tiles_per_seq = seq // TM_FFN
    row = lambda i, j: (i, 0)
    const = lambda i, j: (0, 0)
    return pl.pallas_call(
        functools.partial(_ffn_kernel, sub=sub),
        grid=(n // TM_FFN, D_FF // TF),
        in_specs=[pl.BlockSpec((TM_FFN, d), row),
                  pl.BlockSpec((1, 3 * N_SUBLAYERS, d), lambda i, j: (i // tiles_per_seq, 0, 0)),
                  pl.BlockSpec((1, d), const),
                  pl.BlockSpec((1, d), const),
                  pl.BlockSpec((d, TF), lambda i, j: (0, j)),
                  pl.BlockSpec((d, TF), lambda i, j: (0, j)),
                  pl.BlockSpec((TF, d), lambda i, j: (j, 0))],
        out_specs=pl.BlockSpec((TM_FFN, d), row),
        out_shape=jax.ShapeDtypeStruct((n, d), F32),
        scratch_shapes=[pltpu.VMEM((TM_FFN, d), BF16)],
        compiler_params=pltpu.CompilerParams(dimension_semantics=("parallel", "arbitrary"),
                                             vmem_limit_bytes=VMEM_LIMIT),
        name=f"ffn{sub}",
    )(x, mod, g_pre, g_post, wg, wu, wd)


def _conv_kernel(x_ref, mod_ref, gpre_ref, w_ref, wdw_ref, bdw_ref, lng_ref, lnb_ref, gout_ref,
                 o_ref, u_ref, hp_ref, sh_ref, cv_ref):
    t = pl.program_id(1)

    @pl.when(t == 0)
    def _():
        hp_ref[0:HALO, :] = jnp.zeros((HALO, D_CONV), F32)

    gain, shift = _prenorm_rows(gpre_ref, mod_ref, 1)
    for rows in _row_chunks(TM):
        u = _unit_rms(x_ref[rows, :]) * gain + shift
        u_ref[0, rows, :] = u.reshape(ROW_CH, -1).astype(BF16)
    u = u_ref[0]

    first = HALO - (CONV_WIDTH - 1)
    sh_rows = TM + HALO - SUBLANES
    for cb in range(D_CONV // CONV_CB):
        cols = slice(cb * CONV_CB, (cb + 1) * CONV_CB)
        gcols = slice(D_CONV + cb * CONV_CB, D_CONV + (cb + 1) * CONV_CB)
        val = _dot(u, w_ref[:, cols])
        hp_ref[HALO:HALO + TM, cols] = val * jax.nn.sigmoid(_dot(u, w_ref[:, gcols]))
        for s in range(1, SUBLANES):
            sh_ref[s - 1, :, cols] = hp_ref[s:s + sh_rows, cols]
        for r0 in range(0, TM, CONV_R):
            acc = jnp.zeros((CONV_R // SUBLANES, SUBLANES, CONV_CB), F32)
            for k in range(CONV_WIDTH):
                s, q = (first + k) % SUBLANES, (first + k) // SUBLANES
                start = r0 + q * SUBLANES
                if s == 0:
                    win = hp_ref[start:start + CONV_R, cols]
                else:
                    win = sh_ref[s - 1, start:start + CONV_R, cols]
                win = win.reshape(CONV_R // SUBLANES, SUBLANES, CONV_CB)
                acc = acc + win * wdw_ref[k, :, cols][None]
            cv_ref[r0:r0 + CONV_R, cols] = acc.reshape(CONV_R, CONV_CB)

    hp_ref[0:HALO, :] = hp_ref[TM:TM + HALO, :]

    bias, ln_g, ln_b, g_out = _rep(bdw_ref[...]), _rep(lng_ref[...]), _rep(lnb_ref[...]), _rep(gout_ref[...])
    for rows in _row_chunks(TM):
        cv = cv_ref[rows, :].reshape(ROW_CH // SUBLANES, SUBLANES, D_CONV) + bias
        xc = cv - jnp.mean(cv, axis=-1, keepdims=True)
        var = jnp.mean(xc * xc, axis=-1, keepdims=True)
        y = xc * lax.rsqrt(var + EPS) * ln_g + ln_b
        y = y * jax.nn.sigmoid(y)
        y = y * lax.rsqrt(jnp.mean(y * y, axis=-1, keepdims=True) + EPS) * g_out
        o_ref[0, rows, :] = y.reshape(ROW_CH, D_CONV).astype(BF16)


def _conv_branch(x, mod, g_pre, w_conv, wdw_b, b_dw, ln_g, ln_b, g_out):
    bsz, seq, d = x.shape
    const2 = lambda b, t: (0, 0)
    return pl.pallas_call(
        _conv_kernel,
        grid=(bsz, seq // TM),
        in_specs=[pl.BlockSpec((None, TM, d), lambda b, t: (b, t, 0)),
                  pl.BlockSpec((1, 3 * N_SUBLAYERS, d), lambda b, t: (b, 0, 0)),
                  pl.BlockSpec((1, d), const2),
                  pl.BlockSpec((d, 2 * D_CONV), const2),
                  pl.BlockSpec((CONV_WIDTH, SUBLANES, D_CONV), lambda b, t: (0, 0, 0)),
                  pl.BlockSpec((1, D_CONV), const2),
                  pl.BlockSpec((1, D_CONV), const2),
                  pl.BlockSpec((1, D_CONV), const2),
                  pl.BlockSpec((1, D_CONV), const2)],
        out_specs=[pl.BlockSpec((1, TM, D_CONV), lambda b, t: (b, t, 0)),
                   pl.BlockSpec((1, TM, d), lambda b, t: (b, t, 0))],
        out_shape=[jax.ShapeDtypeStruct((bsz, seq, D_CONV), BF16),
                   jax.ShapeDtypeStruct((bsz, seq, d), BF16)],
        scratch_shapes=[pltpu.VMEM((TM + HALO, D_CONV), F32),
                        pltpu.VMEM((SUBLANES - 1, TM + HALO - SUBLANES, D_CONV), F32),
                        pltpu.VMEM((TM, D_CONV), F32)],
        compiler_params=pltpu.CompilerParams(dimension_semantics=("parallel", "arbitrary"),
                                             vmem_limit_bytes=VMEM_LIMIT),
        name="conv_branch",
    )(x, mod, g_pre, w_conv, wdw_b, b_dw, ln_g, ln_b, g_out)


def _rope(xr, cos, sin_lo, sin_hi):
    return (xr * cos + pltpu.roll(xr, LANES - QK_ROPE_DIM // 2, axis=1) * sin_lo
            + pltpu.roll(xr, QK_ROPE_DIM // 2, axis=1) * sin_hi)


def _mla_proj_kernel(*refs):
    n_q, n_kv = Q_LORA_RANK // MLA_WB, KV_LORA_RANK // MLA_WB
    u_ref, pos_ref, invf_ref = refs[:3]
    wq_refs = refs[3:3 + n_q]
    wkv_refs = refs[3 + n_q:3 + n_q + n_kv]
    (wkr_ref, gq_ref, gkv_ref, wqn_ref, wqr_ref, wuk_ref, wuv_ref,
     qn_ref, qr_ref, kn_ref, kr_ref, v_ref) = refs[3 + n_q + n_kv:]
    u = u_ref[...]

    ang = pos_ref[...].astype(F32) * invf_ref[...]
    cos = jnp.cos(ang)
    sin = jnp.sin(ang)
    lane = lax.broadcasted_iota(jnp.int32, ang.shape, 1)
    half = QK_ROPE_DIM // 2
    sin_lo = jnp.where(lane < half, -sin, 0.0)
    sin_hi = jnp.where((lane >= half) & (lane < QK_ROPE_DIM), sin, 0.0)

    scale = (QK_NOPE_DIM + QK_ROPE_DIM) ** -0.5 * LOG2_E
    qlat = _dot(u, jnp.concatenate([w[...] for w in wq_refs], axis=1))
    kvlat = _dot(u, jnp.concatenate([w[...] for w in wkv_refs], axis=1))
    kr = _dot(u, wkr_ref[...] + _ordering_zero(cos + sin, wkr_ref.shape, BF16))
    qlat = _rms(qlat, gq_ref[...]).astype(BF16)
    qn_ref[...] = (_dot(qlat, wqn_ref[...]) * scale).astype(BF16)
    qr = _dot(qlat, wqr_ref[...])
    ckv = _rms(kvlat, gkv_ref[...]).astype(BF16)
    kn_ref[...] = _dot(ckv, wuk_ref[...]).astype(BF16)
    v_ref[...] = _dot(ckv, wuv_ref[...]).astype(BF16)
    kr_ref[...] = _rope(kr, cos, sin_lo, sin_hi).astype(BF16)
    for h in range(N_HEADS):
        cols = slice(h * LANES, (h + 1) * LANES)
        qr_ref[:, cols] = (_rope(qr[:, cols], cos, sin_lo, sin_hi) * scale).astype(BF16)


def _mla_proj(u, pos, invf, w_in, wkr, gq, gkv, wqn, wqr, wuk, wuv):
    n, d = u.shape
    row = lambda i: (i, 0)
    const = lambda i: (0, 0)
    full = lambda a: pl.BlockSpec(a.shape, const)
    q0, kv0 = 2 * D_CONV, 2 * D_CONV + Q_LORA_RANK
    assert q0 % MLA_WB == 0 and Q_LORA_RANK % MLA_WB == 0 and KV_LORA_RANK % MLA_WB == 0
    col_blocks = list(range(q0 // MLA_WB, (kv0 + KV_LORA_RANK) // MLA_WB))
    w_specs = [pl.BlockSpec((d, MLA_WB), lambda i, c=c: (0, c)) for c in col_blocks]
    wide = jax.ShapeDtypeStruct((n, N_HEADS * LANES), BF16)
    return pl.pallas_call(
        _mla_proj_kernel,
        grid=(n // TM,),
        in_specs=[pl.BlockSpec((TM, d), row),
                  pl.BlockSpec((TM, 1), row),
                  full(invf)] + w_specs + [full(wkr), full(gq), full(gkv),
                                           full(wqn), full(wqr), full(wuk), full(wuv)],
        out_specs=[pl.BlockSpec((TM, N_HEADS * LANES), row),
                   pl.BlockSpec((TM, N_HEADS * LANES), row),
                   pl.BlockSpec((TM, N_HEADS * LANES), row),
                   pl.BlockSpec((TM, LANES), row),
                   pl.BlockSpec((TM, N_HEADS * LANES), row)],
        out_shape=[wide, wide, wide, jax.ShapeDtypeStruct((n, LANES), BF16), wide],
        compiler_params=pltpu.CompilerParams(dimension_semantics=("parallel",),
                                             vmem_limit_bytes=VMEM_LIMIT),
        name="mla_proj",
    )(u, pos, invf, *([w_in] * len(col_blocks)), wkr, gq, gkv, wqn, wqr, wuk, wuv)


def _dot_nt(a, b):
    return lax.dot_general(a, b, (((1,), (1,)), ((), ())), preferred_element_type=F32)


def _attn_kernel(*refs, n_cast):
    qn_ref, qr_ref, kn_ref, kr_ref, v_ref = refs[:5]
    o_ref = refs[5 + n_cast]
    for src, dst in zip(refs[5:5 + n_cast], refs[6 + n_cast:]):
        dst[...] = src[...].astype(BF16)
    seq = o_ref.shape[1]
    ki = lax.broadcasted_iota(jnp.int32, (TQ, TQ), 0)
    qj = lax.broadcasted_iota(jnp.int32, (TQ, TQ), 1)
    v_t = [v_ref[0, :, hh * LANES:(hh + 1) * LANES].astype(F32).T.astype(BF16)
           for hh in range(ATTN_HEADS)]

    def scores(hh, qi):
        hc = slice(hh * LANES, (hh + 1) * LANES)
        diag = slice(qi * TQ, (qi + 1) * TQ)
        past = slice(0, qi * TQ)
        q = jnp.concatenate([qn_ref[0, diag, hc], qr_ref[0, diag, hc]], axis=-1)
        k_d = jnp.concatenate([kn_ref[0, diag, hc], kr_ref[0, diag, :]], axis=-1)
        s_d = jnp.where(ki <= qj, _dot_nt(k_d, q), -jnp.inf)
        if not qi:
            return s_d, None
        k_p = jnp.concatenate([kn_ref[0, past, hc], kr_ref[0, past, :]], axis=-1)
        return s_d, _dot_nt(k_p, q)

    def softmax(s_d, s_p):
        m = jnp.max(s_d, axis=0, keepdims=True)
        if s_p is not None:
            m = jnp.maximum(m, jnp.max(s_p, axis=0, keepdims=True))
        p_d = jnp.exp2(s_d - m)
        l = jnp.sum(p_d, axis=0, keepdims=True)
        if s_p is None:
            return p_d.astype(BF16), None, l
        p_p = jnp.exp2(s_p - m)
        return p_d.astype(BF16), p_p.astype(BF16), l + jnp.sum(p_p, axis=0, keepdims=True)

    def weighted_values(hh, qi, p_d, p_p, l):
        diag = slice(qi * TQ, (qi + 1) * TQ)
        acc = _dot(v_t[hh][:, diag], p_d)
        if p_p is not None:
            acc = acc + _dot(v_t[hh][:, 0:qi * TQ], p_p)
        o_ref[0, diag, hh * LANES:(hh + 1) * LANES] = (acc * (1.0 / l)).T.astype(BF16)

    chains = [(hh, qi) for qi in reversed(range(seq // TQ)) for hh in range(ATTN_HEADS)]
    s_vals, p_vals = {}, {}
    for t in range(len(chains) + 2):
        if t < len(chains):
            s_vals[t] = scores(*chains[t])
        if 0 <= t - 1 < len(chains):
            p_vals[t - 1] = softmax(*s_vals.pop(t - 1))
        if 0 <= t - 2 < len(chains):
            weighted_values(*chains[t - 2], *p_vals.pop(t - 2))


def _cast_row_spec(a, n_steps):
    rows, repeat = a.shape[0], 1
    while rows % (n_steps // repeat) or (rows // (n_steps // repeat)) % (2 * SUBLANES):
        repeat *= 2
    groups = N_HEADS // ATTN_HEADS
    return pl.BlockSpec((rows // (n_steps // repeat), a.shape[1]),
                        lambda b, h: ((b * groups + h) // repeat, 0))


def _attention(qn, qr, kn, kr, v, cast=()):
    bsz, seq, _ = qn.shape
    groups = N_HEADS // ATTN_HEADS
    head = pl.BlockSpec((1, seq, ATTN_HEADS * LANES), lambda b, h: (b, 0, h))
    cast_specs = [_cast_row_spec(a, bsz * groups) for a in cast]
    return pl.pallas_call(
        functools.partial(_attn_kernel, n_cast=len(cast)),
        grid=(bsz, groups),
        in_specs=[head, head, head, pl.BlockSpec((1, seq, LANES), lambda b, h: (b, 0, 0)), head]
                 + cast_specs,
        out_specs=[head] + cast_specs,
        out_shape=[jax.ShapeDtypeStruct((bsz, seq, D_ATTN), BF16)]
                  + [jax.ShapeDtypeStruct(a.shape, BF16) for a in cast],
        compiler_params=pltpu.CompilerParams(dimension_semantics=("arbitrary", "arbitrary"),
                                             vmem_limit_bytes=VMEM_LIMIT),
        name="mla_attn",
    )(qn, qr, kn, kr, v, *cast)


def _mix_out_kernel(x_ref, mod_ref, cv_ref, at_ref, gattn_ref, wc_ref, wa_ref, gpost_ref, o_ref,
                    an_ref, y_ref):
    g_attn = _rep(gattn_ref[...])
    for rows in _row_chunks(TM):
        a = _unit_rms(at_ref[rows, :].astype(F32)) * g_attn
        an_ref[rows, :] = a.reshape(ROW_CH, -1).astype(BF16)
    y_ref[...] = _dot(cv_ref[...], wc_ref[...]) + _dot(an_ref[...], wa_ref[...])
    gate = _rep(mod_ref[0, 5:6, :] * gpost_ref[...])
    for rows in _row_chunks(TM):
        y = (_unit_rms(y_ref[rows, :]) * gate).reshape(ROW_CH, -1)
        o_ref[rows, :] = x_ref[rows, :] + y


def _mix_out(x, mod, cv, at, g_attn, w_out, g_post, *, seq):
    n, d = x.shape
    tiles_per_seq = seq // TM
    row = lambda i: (i, 0)
    const = lambda i: (0, 0)
    return pl.pallas_call(
        _mix_out_kernel,
        grid=(n // TM,),
        in_specs=[pl.BlockSpec((TM, d), row),
                  pl.BlockSpec((1, 3 * N_SUBLAYERS, d), lambda i: (i // tiles_per_seq, 0, 0)),
                  pl.BlockSpec((TM, D_CONV), row),
                  pl.BlockSpec((TM, D_ATTN), row),
                  pl.BlockSpec((1, D_ATTN), const),
                  pl.BlockSpec((D_CONV, d), const),
                  pl.BlockSpec((D_ATTN, d), lambda i: (D_CONV // D_ATTN, 0)),
                  pl.BlockSpec((1, d), const)],
        out_specs=pl.BlockSpec((TM, d), row),
        out_shape=jax.ShapeDtypeStruct((n, d), F32),
        scratch_shapes=[pltpu.VMEM((TM, D_ATTN), BF16), pltpu.VMEM((TM, d), F32)],
        compiler_params=pltpu.CompilerParams(dimension_semantics=("parallel",),
                                             vmem_limit_bytes=VMEM_LIMIT),
        name="mix_out",
    )(x, mod, cv, at, g_attn, w_out, w_out, g_post)


def _rope_columns(w, n_groups, group, lo):
    k = w.shape[0]
    w = w.reshape(k, n_groups, group)[:, :, lo:lo + QK_ROPE_DIM]
    w = jnp.pad(w, ((0, 0), (0, 0), (0, LANES - QK_ROPE_DIM)))
    return w.reshape(k, n_groups * LANES)


def kernel(x, c, positions, w_ada, b_ada, g_pre_ffn1, w1_gate, w1_up, w1_down, g_post_ffn1, g_pre_mix, w_in, w_dw, b_dw, ln_conv_g, ln_conv_b, g_q_lat, w_uq, g_kv_lat, w_uk, w_uv, g_conv_out, g_attn_out, w_out, g_post_mix, g_pre_ffn2, w2_gate, w2_up, w2_down, g_post_ffn2):
    bsz, seq, d = x.shape
    n = bsz * seq
    depth = w_ada.shape[0]
    half = QK_ROPE_DIM // 2
    inv_freq = ROPE_BASE ** (-jnp.arange(half, dtype=F32) / half)
    invf = jnp.concatenate([inv_freq, inv_freq, jnp.zeros((LANES - QK_ROPE_DIM,), F32)])[None, :]
    pos = positions.reshape(n, 1)
    bf = lambda a: a.astype(BF16)

    xf = x.reshape(n, d)
    for l in range(depth):
        mod = _adaln(c, w_ada[l], b_ada[l][None, :]).reshape(bsz, 3 * N_SUBLAYERS, d)

        xf = _ffn(xf, mod, g_pre_ffn1[l][None], g_post_ffn1[l][None],
                  bf(w1_gate[l]), bf(w1_up[l]), bf(w1_down[l]), sub=0, seq=seq)

        i1 = 2 * D_CONV
        i2 = i1 + Q_LORA_RANK
        i3 = i2 + KV_LORA_RANK
        wi = w_in[l]
        wi_bf = bf(wi)
        cv, u = _conv_branch(xf.reshape(bsz, seq, d), mod, g_pre_mix[l][None], wi_bf,
                             jnp.broadcast_to(w_dw[l][:, None, :], (CONV_WIDTH, SUBLANES, D_CONV)),
                             b_dw[l][None], ln_conv_g[l][None], ln_conv_b[l][None], g_conv_out[l][None])

        qk = QK_NOPE_DIM + QK_ROPE_DIM
        wqn = w_uq[l].reshape(Q_LORA_RANK, N_HEADS, qk)[:, :, :QK_NOPE_DIM].reshape(Q_LORA_RANK, -1)
        wqr = _rope_columns(w_uq[l], N_HEADS, qk, QK_NOPE_DIM)
        wkr = _rope_columns(wi[:, i3:], 1, QK_ROPE_DIM, 0)
        qn, qr, kn, kr, v = _mla_proj(
            u.reshape(n, d), pos, invf, wi_bf, bf(wkr),
            g_q_lat[l][None], g_kv_lat[l][None], bf(wqn), bf(wqr), bf(w_uk[l]), bf(w_uv[l]))

        shp = lambda a: a.reshape(bsz, seq, a.shape[-1])
        at, w2g, w2u, w2d, wo = _attention(shp(qn), shp(qr), shp(kn), shp(kr), shp(v),
                                           cast=(w2_gate[l], w2_up[l], w2_down[l], w_out[l]))

        xf = _mix_out(xf, mod, cv.reshape(n, D_CONV), at.reshape(n, D_ATTN), g_attn_out[l][None],
                      wo, g_post_mix[l][None], seq=seq)

        xf = _ffn(xf, mod, g_pre_ffn2[l][None], g_post_ffn2[l][None], w2g, w2u, w2d, sub=2, seq=seq)
    return xf.reshape(bsz, seq, d)
```

```python
import functools

import jax
import jax.numpy as jnp
from jax import lax
from jax.experimental import pallas as pl
from jax.experimental.pallas import tpu as pltpu

D_MODEL = 2048
D_CONV = 1024
CONV_WIDTH = 31
N_HEADS = 8
QK_NOPE_DIM = 128
QK_ROPE_DIM = 64
V_HEAD_DIM = 128
Q_LORA_RANK = 768
KV_LORA_RANK = 512
D_ATTN = N_HEADS * V_HEAD_DIM
D_FF = 5632
FFN_RES_WEIGHT = 0.5
N_SUBLAYERS = 3
ROPE_BASE = 10000.0
EPS = 1e-6
LOG2_E = 1.4426950408889634

LANES = 128
SUBLANES = 8
VMEM_LIMIT = 56 * 1024 * 1024

TM = 512
TM_FFN = 1024
TF = 512
TN_ADA = 1024
TQ = 512
ATTN_HEADS = 2
MLA_WB = 256
HALO = 32
CONV_CB = 256
CONV_R = 64
ROW_CH = 16

F32 = jnp.float32
BF16 = jnp.bfloat16


def _rms(x, g):
    return x * lax.rsqrt(jnp.mean(x * x, axis=-1, keepdims=True) + EPS) * g


def _rep(row):
    return jnp.broadcast_to(row, (SUBLANES, row.shape[-1]))


def _unit_rms(x):
    r, d = x.shape
    x = x.reshape(r // SUBLANES, SUBLANES, d)
    return x * lax.rsqrt(jnp.mean(x * x, axis=-1, keepdims=True) + EPS)


def _prenorm_rows(g_ref, mod_ref, sub):
    shift = mod_ref[0, 3 * sub:3 * sub + 1, :]
    scale = mod_ref[0, 3 * sub + 1:3 * sub + 2, :]
    return _rep(g_ref[...] * (1.0 + scale)), _rep(shift)


def _dot(a, b):
    return jnp.dot(a, b, preferred_element_type=F32)


def _row_chunks(n_rows):
    return [slice(r, r + ROW_CH) for r in range(0, n_rows, ROW_CH)]


def _ordering_zero(v, shape, dtype):
    rows, lanes = v.shape
    t = jnp.sum(v.reshape(rows // SUBLANES, SUBLANES, lanes), axis=0)
    bits = lax.shift_right_logical(lax.shift_right_logical(pltpu.bitcast(t, jnp.int32), 16), 16)
    return jnp.tile(bits.astype(F32), (shape[0] // SUBLANES, shape[1] // lanes)).astype(dtype)


def _adaln_kernel(c_ref, w_ref, b_ref, o_ref):
    c = c_ref[...]
    sc = (c * jax.nn.sigmoid(c)).astype(BF16)
    o_ref[...] = _dot(sc, w_ref[...].astype(BF16)) + b_ref[...]


def _adaln(c, w, b):
    bsz, d = c.shape
    n = w.shape[1]
    return pl.pallas_call(
        _adaln_kernel,
        grid=(n // TN_ADA,),
        in_specs=[pl.BlockSpec((bsz, d), lambda j: (0, 0)),
                  pl.BlockSpec((d, TN_ADA), lambda j: (0, j)),
                  pl.BlockSpec((1, TN_ADA), lambda j: (0, j))],
        out_specs=pl.BlockSpec((bsz, TN_ADA), lambda j: (0, j)),
        out_shape=jax.ShapeDtypeStruct((bsz, n), F32),
        compiler_params=pltpu.CompilerParams(dimension_semantics=("arbitrary",),
                                             vmem_limit_bytes=VMEM_LIMIT),
        name="adaln_mod",
    )(c, w, b)


def _ffn_kernel(x_ref, mod_ref, gpre_ref, gpost_ref, wg_ref, wu_ref, wd_ref, o_ref, u_ref, *, sub):
    j = pl.program_id(1)

    def swiglu_block():
        u = u_ref[...]
        g = _dot(u, wg_ref[...])
        up = _dot(u, wu_ref[...])
        h = (g * jax.nn.sigmoid(g) * up).astype(BF16)
        return _dot(h, wd_ref[...])

    @pl.when(j == 0)
    def _():
        gain, shift = _prenorm_rows(gpre_ref, mod_ref, sub)
        for rows in _row_chunks(TM_FFN):
            u = _unit_rms(x_ref[rows, :]) * gain + shift
            u_ref[rows, :] = u.reshape(ROW_CH, -1).astype(BF16)
        o_ref[...] = swiglu_block()

    last = pl.num_programs(1) - 1

    @pl.when((j > 0) & (j < last))
    def _():
        o_ref[...] += swiglu_block()

    @pl.when(j == last)
    def _():
        o_ref[...] += swiglu_block()
        gate = _rep(FFN_RES_WEIGHT * mod_ref[0, 3 * sub + 2:3 * sub + 3, :] * gpost_ref[...])
        for rows in _row_chunks(TM_FFN):
            y = (_unit_rms(o_ref[rows, :]) * gate).reshape(ROW_CH, -1)
            o_ref[rows, :] = x_ref[rows, :] + y


def _ffn(x, mod, g_pre, g_post, wg, wu, wd, *, sub, seq):
    n, d = x.shape
    tiles_per_seq = seq // TM_FFN
    row = lambda i, j: (i, 0)
    const = lambda i, j: (0, 0)
    return pl.pallas_call(
        functools.partial(_ffn_kernel, sub=sub),
        grid=(n // TM_FFN, D_FF // TF),
        in_specs=[pl.BlockSpec((TM_FFN, d), row),
                  pl.BlockSpec((1, 3 * N_SUBLAYERS, d), lambda i, j: (i // tiles_per_seq, 0, 0)),
                  pl.BlockSpec((1, d), const),
                  pl.BlockSpec((1, d), const),
                  pl.BlockSpec((d, TF), lambda i, j: (0, j)),
                  pl.BlockSpec((d, TF), lambda i, j: (0, j)),
                  pl.BlockSpec((TF, d), lambda i, j: (j, 0))],
        out_specs=pl.BlockSpec((TM_FFN, d), row),
        out_shape=jax.ShapeDtypeStruct((n, d), F32),
        scratch_shapes=[pltpu.VMEM((TM_FFN, d), BF16)],
        compiler_params=pltpu.CompilerParams(dimension_semantics=("parallel", "arbitrary"),
                                             vmem_limit_bytes=VMEM_LIMIT),
        name=f"ffn{sub}",
    )(x, mod, g_pre, g_post, wg, wu, wd)


def _conv_kernel(x_ref, mod_ref, gpre_ref, w_ref, wdw_ref, bdw_ref, lng_ref, lnb_ref, gout_ref,
                 o_ref, u_ref, hp_ref, sh_ref, cv_ref):
    t = pl.program_id(1)

    @pl.when(t == 0)
    def _():
        hp_ref[0:HALO, :] = jnp.zeros((HALO, D_CONV), F32)

    gain, shift = _prenorm_rows(gpre_ref, mod_ref, 1)
    for rows in _row_chunks(TM):
        u = _unit_rms(x_ref[rows, :]) * gain + shift
        u_ref[0, rows, :] = u.reshape(ROW_CH, -1).astype(BF16)
    u = u_ref[0]

    first = HALO - (CONV_WIDTH - 1)
    sh_rows = TM + HALO - SUBLANES
    for cb in range(D_CONV // CONV_CB):
        cols = slice(cb * CONV_CB, (cb + 1) * CONV_CB)
        gcols = slice(D_CONV + cb * CONV_CB, D_CONV + (cb + 1) * CONV_CB)
        val = _dot(u, w_ref[:, cols])
        hp_ref[HALO:HALO + TM, cols] = val * jax.nn.sigmoid(_dot(u, w_ref[:, gcols]))
        for s in range(1, SUBLANES):
            sh_ref[s - 1, :, cols] = hp_ref[s:s + sh_rows, cols]
        for r0 in range(0, TM, CONV_R):
            acc = jnp.zeros((CONV_R // SUBLANES, SUBLANES, CONV_CB), F32)
            for k in range(CONV_WIDTH):
                s, q = (first + k) % SUBLANES, (first + k) // SUBLANES
                start = r0 + q * SUBLANES
                if s == 0:
                    win = hp_ref[start:start + CONV_R, cols]
                else:
                    win = sh_ref[s - 1, start:start + CONV_R, cols]
                win = win.reshape(CONV_R // SUBLANES, SUBLANES, CONV_CB)
                acc = acc + win * wdw_ref[k, :, cols][None]
            cv_ref[r0:r0 + CONV_R, cols] = acc.reshape(CONV_R, CONV_CB)

    hp_ref[0:HALO, :] = hp_ref[TM:TM + HALO, :]

    bias, ln_g, ln_b, g_out = _rep(bdw_ref[...]), _rep(lng_ref[...]), _rep(lnb_ref[...]), _rep(gout_ref[...])
    for rows in _row_chunks(TM):
        cv = cv_ref[rows, :].reshape(ROW_CH // SUBLANES, SUBLANES, D_CONV) + bias
        xc = cv - jnp.mean(cv, axis=-1, keepdims=True)
        var = jnp.mean(xc * xc, axis=-1, keepdims=True)
        y = xc * lax.rsqrt(var + EPS) * ln_g + ln_b
        y = y * jax.nn.sigmoid(y)
        y = y * lax.rsqrt(jnp.mean(y * y, axis=-1, keepdims=True) + EPS) * g_out
        o_ref[0, rows, :] = y.reshape(ROW_CH, D_CONV).astype(BF16)


def _conv_branch(x, mod, g_pre, w_conv, wdw_b, b_dw, ln_g, ln_b, g_out):
    bsz, seq, d = x.shape
    const2 = lambda b, t: (0, 0)
    return pl.pallas_call(
        _conv_kernel,
        grid=(bsz, seq // TM),
        in_specs=[pl.BlockSpec((None, TM, d), lambda b, t: (b, t, 0)),
                  pl.BlockSpec((1, 3 * N_SUBLAYERS, d), lambda b, t: (b, 0, 0)),
                  pl.BlockSpec((1, d), const2),
                  pl.BlockSpec((d, 2 * D_CONV), const2),
                  pl.BlockSpec((CONV_WIDTH, SUBLANES, D_CONV), lambda b, t: (0, 0, 0)),
                  pl.BlockSpec((1, D_CONV), const2),
                  pl.BlockSpec((1, D_CONV), const2),
                  pl.BlockSpec((1, D_CONV), const2),
                  pl.BlockSpec((1, D_CONV), const2)],
        out_specs=[pl.BlockSpec((1, TM, D_CONV), lambda b, t: (b, t, 0)),
                   pl.BlockSpec((1, TM, d), lambda b, t: (b, t, 0))],
        out_shape=[jax.ShapeDtypeStruct((bsz, seq, D_CONV), BF16),
                   jax.ShapeDtypeStruct((bsz, seq, d), BF16)],
        scratch_shapes=[pltpu.VMEM((TM + HALO, D_CONV), F32),
                        pltpu.VMEM((SUBLANES - 1, TM + HALO - SUBLANES, D_CONV), F32),
                        pltpu.VMEM((TM, D_CONV), F32)],
        compiler_params=pltpu.CompilerParams(dimension_semantics=("parallel", "arbitrary"),
                                             vmem_limit_bytes=VMEM_LIMIT),
        name="conv_branch",
    )(x, mod, g_pre, w_conv, wdw_b, b_dw, ln_g, ln_b, g_out)


def _rope(xr, cos, sin_lo, sin_hi):
    return (xr * cos + pltpu.roll(xr, LANES - QK_ROPE_DIM // 2, axis=1) * sin_lo
            + pltpu.roll(xr, QK_ROPE_DIM // 2, axis=1) * sin_hi)


def _mla_proj_kernel(*refs):
    n_q, n_kv = Q_LORA_RANK // MLA_WB, KV_LORA_RANK // MLA_WB
    u_ref, pos_ref, invf_ref = refs[:3]
    wq_refs = refs[3:3 + n_q]
    wkv_refs = refs[3 + n_q:3 + n_q + n_kv]
    (wkr_ref, gq_ref, gkv_ref, wqn_ref, wqr_ref, wuk_ref, wuv_ref,
     qn_ref, qr_ref, kn_ref, kr_ref, v_ref) = refs[3 + n_q + n_kv:]
    u = u_ref[...]

    ang = pos_ref[...].astype(F32) * invf_ref[...]
    cos = jnp.cos(ang)
    sin = jnp.sin(ang)
    lane = lax.broadcasted_iota(jnp.int32, ang.shape, 1)
    half = QK_ROPE_DIM // 2
    sin_lo = jnp.where(lane < half, -sin, 0.0)
    sin_hi = jnp.where((lane >= half) & (lane < QK_ROPE_DIM), sin, 0.0)

    scale = (QK_NOPE_DIM + QK_ROPE_DIM) ** -0.5 * LOG2_E
    qlat = _dot(u, jnp.concatenate([w[...] for w in wq_refs], axis=1))
    kvlat = _dot(u, jnp.concatenate([w[...] for w in wkv_refs], axis=1))
    kr = _dot(u, wkr_ref[...] + _ordering_zero(cos + sin, wkr_ref.shape, BF16))
    qlat = _rms(qlat, gq_ref[...]).astype(BF16)
    qn_ref[...] = (_dot(qlat, wqn_ref[...]) * scale).astype(BF16)
    qr = _dot(qlat, wqr_ref[...])
    ckv = _rms(kvlat, gkv_ref[...]).astype(BF16)
    kn_ref[...] = _dot(ckv, wuk_ref[...]).astype(BF16)
    v_ref[...] = _dot(ckv, wuv_ref[...]).astype(BF16)
    kr_ref[...] = _rope(kr, cos, sin_lo, sin_hi).astype(BF16)
    for h in range(N_HEADS):
        cols = slice(h * LANES, (h + 1) * LANES)
        qr_ref[:, cols] = (_rope(qr[:, cols], cos, sin_lo, sin_hi) * scale).astype(BF16)


def _mla_proj(u, pos, invf, w_in, wkr, gq, gkv, wqn, wqr, wuk, wuv):
    n, d = u.shape
    row = lambda i: (i, 0)
    const = lambda i: (0, 0)
    full = lambda a: pl.BlockSpec(a.shape, const)
    q0, kv0 = 2 * D_CONV, 2 * D_CONV + Q_LORA_RANK
    assert q0 % MLA_WB == 0 and Q_LORA_RANK % MLA_WB == 0 and KV_LORA_RANK % MLA_WB == 0
    col_blocks = list(range(q0 // MLA_WB, (kv0 + KV_LORA_RANK) // MLA_WB))
    w_specs = [pl.BlockSpec((d, MLA_WB), lambda i, c=c: (0, c)) for c in col_blocks]
    wide = jax.ShapeDtypeStruct((n, N_HEADS * LANES), BF16)
    return pl.pallas_call(
        _mla_proj_kernel,
        grid=(n // TM,),
        in_specs=[pl.BlockSpec((TM, d), row),
                  pl.BlockSpec((TM, 1), row),
                  full(invf)] + w_specs + [full(wkr), full(gq), full(gkv),
                                           full(wqn), full(wqr), full(wuk), full(wuv)],
        out_specs=[pl.BlockSpec((TM, N_HEADS * LANES), row),
                   pl.BlockSpec((TM, N_HEADS * LANES), row),
                   pl.BlockSpec((TM, N_HEADS * LANES), row),
                   pl.BlockSpec((TM, LANES), row),
                   pl.BlockSpec((TM, N_HEADS * LANES), row)],
        out_shape=[wide, wide, wide, jax.ShapeDtypeStruct((n, LANES), BF16), wide],
        compiler_params=pltpu.CompilerParams(dimension_semantics=("parallel",),
                                             vmem_limit_bytes=VMEM_LIMIT),
        name="mla_proj",
    )(u, pos, invf, *([w_in] * len(col_blocks)), wkr, gq, gkv, wqn, wqr, wuk, wuv)


def _dot_nt(a, b):
    return lax.dot_general(a, b, (((1,), (1,)), ((), ())), preferred_element_type=F32)


def _attn_kernel(*refs, n_cast):
    qn_ref, qr_ref, kn_ref, kr_ref, v_ref = refs[:5]
    o_ref = refs[5 + n_cast]
    for src, dst in zip(refs[5:5 + n_cast], refs[6 + n_cast:]):
        dst[...] = src[...].astype(BF16)
    seq = o_ref.shape[1]
    ki = lax.broadcasted_iota(jnp.int32, (TQ, TQ), 0)
    qj = lax.broadcasted_iota(jnp.int32, (TQ, TQ), 1)
    v_t = [v_ref[0, :, hh * LANES:(hh + 1) * LANES].astype(F32).T.astype(BF16)
           for hh in range(ATTN_HEADS)]

    def scores(hh, qi):
        hc = slice(hh * LANES, (hh + 1) * LANES)
        diag = slice(qi * TQ, (qi + 1) * TQ)
        past = slice(0, qi * TQ)
        q = jnp.concatenate([qn_ref[0, diag, hc], qr_ref[0, diag, hc]], axis=-1)
        k_d = jnp.concatenate([kn_ref[0, diag, hc], kr_ref[0, diag, :]], axis=-1)
        s_d = jnp.where(ki <= qj, _dot_nt(k_d, q), -jnp.inf)
        if not qi:
            return s_d, None
        k_p = jnp.concatenate([kn_ref[0, past, hc], kr_ref[0, past, :]], axis=-1)
        return s_d, _dot_nt(k_p, q)

    def softmax_cols(s_d, s_p):
        m = jnp.max(s_d, axis=0, keepdims=True)
        if s_p is not None:
            m = jnp.maximum(m, jnp.max(s_p, axis=0, keepdims=True))
        p_d = jnp.exp2(s_d - m)
        l = jnp.sum(p_d, axis=0, keepdims=True)
        if s_p is None:
            return p_d.astype(BF16), None, l
        p_p = jnp.exp2(s_p - m)
        return p_d.astype(BF16), p_p.astype(BF16), l + jnp.sum(p_p, axis=0, keepdims=True)

    def softmax(s_d, s_p):
        halves = [softmax_cols(s_d[:, c:c + TQ // 2], None if s_p is None else s_p[:, c:c + TQ // 2])
                  for c in (0, TQ // 2)]
        p_d, p_p, l = (None if a is None else jnp.concatenate([a, b], axis=1) for a, b in zip(*halves))
        return p_d, p_p, l

    def weighted_values(hh, qi, p_d, p_p, l):
        diag = slice(qi * TQ, (qi + 1) * TQ)
        acc = _dot(v_t[hh][:, diag], p_d)
        if p_p is not None:
            acc = acc + _dot(v_t[hh][:, 0:qi * TQ], p_p)
        o_ref[0, diag, hh * LANES:(hh + 1) * LANES] = (acc * (1.0 / l)).T.astype(BF16)

    chains = [(hh, qi) for qi in reversed(range(seq // TQ)) for hh in range(ATTN_HEADS)]
    s_vals, p_vals = {}, {}
    for t in range(len(chains) + 2):
        if t < len(chains):
            s_vals[t] = scores(*chains[t])
        if 0 <= t - 1 < len(chains):
            p_vals[t - 1] = softmax(*s_vals.pop(t - 1))
        if 0 <= t - 2 < len(chains):
            weighted_values(*chains[t - 2], *p_vals.pop(t - 2))


def _cast_row_spec(a, n_steps):
    rows, repeat = a.shape[0], 1
    while rows % (n_steps // repeat) or (rows // (n_steps // repeat)) % (2 * SUBLANES):
        repeat *= 2
    groups = N_HEADS // ATTN_HEADS
    return pl.BlockSpec((rows // (n_steps // repeat), a.shape[1]),
                        lambda b, h: ((b * groups + h) // repeat, 0))


def _attention(qn, qr, kn, kr, v, cast=()):
    bsz, seq, _ = qn.shape
    groups = N_HEADS // ATTN_HEADS
    head = pl.BlockSpec((1, seq, ATTN_HEADS * LANES), lambda b, h: (b, 0, h))
    cast_specs = [_cast_row_spec(a, bsz * groups) for a in cast]
    return pl.pallas_call(
        functools.partial(_attn_kernel, n_cast=len(cast)),
        grid=(bsz, groups),
        in_specs=[head, head, head, pl.BlockSpec((1, seq, LANES), lambda b, h: (b, 0, 0)), head]
                 + cast_specs,
        out_specs=[head] + cast_specs,
        out_shape=[jax.ShapeDtypeStruct((bsz, seq, D_ATTN), BF16)]
                  + [jax.ShapeDtypeStruct(a.shape, BF16) for a in cast],
        compiler_params=pltpu.CompilerParams(dimension_semantics=("arbitrary", "arbitrary"),
                                             vmem_limit_bytes=VMEM_LIMIT),
        name="mla_attn",
    )(qn, qr, kn, kr, v, *cast)


def _mix_out_kernel(x_ref, mod_ref, cv_ref, at_ref, gattn_ref, wc_ref, wa_ref, gpost_ref, o_ref,
                    an_ref, y_ref):
    g_attn = _rep(gattn_ref[...])
    for rows in _row_chunks(TM):
        a = _unit_rms(at_ref[rows, :].astype(F32)) * g_attn
        an_ref[rows, :] = a.reshape(ROW_CH, -1).astype(BF16)
    y_ref[...] = _dot(cv_ref[...], wc_ref[...]) + _dot(an_ref[...], wa_ref[...])
    gate = _rep(mod_ref[0, 5:6, :] * gpost_ref[...])
    for rows in _row_chunks(TM):
        y = (_unit_rms(y_ref[rows, :]) * gate).reshape(ROW_CH, -1)
        o_ref[rows, :] = x_ref[rows, :] + y


def _mix_out(x, mod, cv, at, g_attn, w_out, g_post, *, seq):
    n, d = x.shape
    tiles_per_seq = seq // TM
    row = lambda i: (i, 0)
    const = lambda i: (0, 0)
    return pl.pallas_call(
        _mix_out_kernel,
        grid=(n // TM,),
        in_specs=[pl.BlockSpec((TM, d), row),
                  pl.BlockSpec((1, 3 * N_SUBLAYERS, d), lambda i: (i // tiles_per_seq, 0, 0)),
                  pl.BlockSpec((TM, D_CONV), row),
                  pl.BlockSpec((TM, D_ATTN), row),
                  pl.BlockSpec((1, D_ATTN), const),
                  pl.BlockSpec((D_CONV, d), const),
                  pl.BlockSpec((D_ATTN, d), lambda i: (D_CONV // D_ATTN, 0)),
                  pl.BlockSpec((1, d), const)],
        out_specs=pl.BlockSpec((TM, d), row),
        out_shape=jax.ShapeDtypeStruct((n, d), F32),
        scratch_shapes=[pltpu.VMEM((TM, D_ATTN), BF16), pltpu.VMEM((TM, d), F32)],
        compiler_params=pltpu.CompilerParams(dimension_semantics=("parallel",),
                                             vmem_limit_bytes=VMEM_LIMIT),
        name="mix_out",
    )(x, mod, cv, at, g_attn, w_out, w_out, g_post)


def _rope_columns(w, n_groups, group, lo):
    k = w.shape[0]
    w = w.reshape(k, n_groups, group)[:, :, lo:lo + QK_ROPE_DIM]
    w = jnp.pad(w, ((0, 0), (0, 0), (0, LANES - QK_ROPE_DIM)))
    return w.reshape(k, n_groups * LANES)


def kernel(x, c, positions, w_ada, b_ada, g_pre_ffn1, w1_gate, w1_up, w1_down, g_post_ffn1, g_pre_mix, w_in, w_dw, b_dw, ln_conv_g, ln_conv_b, g_q_lat, w_uq, g_kv_lat, w_uk, w_uv, g_conv_out, g_attn_out, w_out, g_post_mix, g_pre_ffn2, w2_gate, w2_up, w2_down, g_post_ffn2):
    bsz, seq, d = x.shape
    n = bsz * seq
    depth = w_ada.shape[0]
    half = QK_ROPE_DIM // 2
    inv_freq = ROPE_BASE ** (-jnp.arange(half, dtype=F32) / half)
    invf = jnp.concatenate([inv_freq, inv_freq, jnp.zeros((LANES - QK_ROPE_DIM,), F32)])[None, :]
    pos = positions.reshape(n, 1)
    bf = lambda a: a.astype(BF16)

    xf = x.reshape(n, d)
    for l in range(depth):
        mod = _adaln(c, w_ada[l], b_ada[l][None, :]).reshape(bsz, 3 * N_SUBLAYERS, d)

        xf = _ffn(xf, mod, g_pre_ffn1[l][None], g_post_ffn1[l][None],
                  bf(w1_gate[l]), bf(w1_up[l]), bf(w1_down[l]), sub=0, seq=seq)

        i1 = 2 * D_CONV
        i2 = i1 + Q_LORA_RANK
        i3 = i2 + KV_LORA_RANK
        wi = w_in[l]
        wi_bf = bf(wi)
        cv, u = _conv_branch(xf.reshape(bsz, seq, d), mod, g_pre_mix[l][None], wi_bf,
                             jnp.broadcast_to(w_dw[l][:, None, :], (CONV_WIDTH, SUBLANES, D_CONV)),
                             b_dw[l][None], ln_conv_g[l][None], ln_conv_b[l][None], g_conv_out[l][None])

        qk = QK_NOPE_DIM + QK_ROPE_DIM
        wqn = w_uq[l].reshape(Q_LORA_RANK, N_HEADS, qk)[:, :, :QK_NOPE_DIM].reshape(Q_LORA_RANK, -1)
        wqr = _rope_columns(w_uq[l], N_HEADS, qk, QK_NOPE_DIM)
        wkr = _rope_columns(wi[:, i3:], 1, QK_ROPE_DIM, 0)
        qn, qr, kn, kr, v = _mla_proj(
            u.reshape(n, d), pos, invf, wi_bf, bf(wkr),
            g_q_lat[l][None], g_kv_lat[l][None], bf(wqn), bf(wqr), bf(w_uk[l]), bf(w_uv[l]))

        shp = lambda a: a.reshape(bsz, seq, a.shape[-1])
        at, w2g, w2u, w2d, wo = _attention(shp(qn), shp(qr), shp(kn), shp(kr), shp(v),
                                           cast=(w2_gate[l], w2_up[l], w2_down[l], w_out[l]))

        xf = _mix_out(xf, mod, cv.reshape(n, D_CONV), at.reshape(n, D_ATTN), g_attn_out[l][None],
                      wo, g_post_mix[l][None], seq=seq)

        xf = _ffn(xf, mod, g_pre_ffn2[l][None], g_post_ffn2[l][None], w2g, w2u, w2d, sub=2, seq=seq)
    return xf.reshape(bsz, seq, d)
```
